```python
import jax, jax.numpy as jnp
from jax import lax
import numpy as np

D_MODEL = 1024
BATCH = 4
SEQ = 8192
DEPTH = 1

GRID_W = 64
N_Q_HEADS = 8
N_KV_HEADS = 2
HEAD_DIM = 64
ATTN_WIDTH = N_Q_HEADS * HEAD_DIM
KV_WIDTH = N_KV_HEADS * HEAD_DIM
Q_BLOCK = 128
ROPE_THETA = 10000.0
DN_HEADS = 8
DN_HEAD_DIM = 64
DN_WIDTH = DN_HEADS * DN_HEAD_DIM
CONV_WIDTH = 5
CHUNK = 64
N_BRANCHES = 2
N_EXPERTS = 16
CAPACITY_FACTOR = 2
D_EXPERT = 1024
EPS = 1e-6
IN_SPLITS = (ATTN_WIDTH, KV_WIDTH, KV_WIDTH, DN_WIDTH, DN_WIDTH, DN_WIDTH, DN_WIDTH, 2 * DN_HEADS, 2 * DN_HEADS, N_BRANCHES * D_MODEL)
D_IN = sum(IN_SPLITS)

kernel_name = "hybrid_gqa_gdn_ec_moe_block"


def rmsnorm(x, w):
    xf = x.astype(jnp.float32)
    y = xf * lax.rsqrt(jnp.mean(xf * xf, axis=-1, keepdims=True) + EPS)
    return (y * w.astype(jnp.float32)).astype(x.dtype)


def l2norm(x):
    xf = x.astype(jnp.float32)
    return xf * lax.rsqrt(jnp.sum(xf * xf, axis=-1, keepdims=True) + EPS)


def heads(t, n):
    B, S, _ = t.shape
    return t.reshape(B, S, n, -1).transpose(0, 2, 1, 3)


def rope_axis(x, pos):
    d = x.shape[-1]
    freqs = ROPE_THETA ** (-(jnp.arange(d // 2, dtype=jnp.float32) * 2.0 / d))
    ang = pos.astype(jnp.float32)[:, None] * freqs[None, :]
    cos = jnp.concatenate([jnp.cos(ang), jnp.cos(ang)], axis=-1)
    sin = jnp.concatenate([jnp.sin(ang), jnp.sin(ang)], axis=-1)
    xf = x.astype(jnp.float32)
    x1, x2 = xf[..., : d // 2], xf[..., d // 2 :]
    rot = jnp.concatenate([-x2, x1], axis=-1)
    return (xf * cos + rot * sin).astype(x.dtype)


def axial_rope(x, row, col):
    hd = x.shape[-1] // 2
    return jnp.concatenate([rope_axis(x[..., :hd], row), rope_axis(x[..., hd:], col)], axis=-1)


def block_attention(q, k, v):
    B, Hq, S, dh = q.shape
    Hkv = k.shape[1]
    G = Hq // Hkv
    nb = S // Q_BLOCK
    qb = q.reshape(B, Hkv, G, nb, Q_BLOCK, dh).transpose(3, 0, 1, 2, 4, 5).astype(jnp.float32)
    kf = k.astype(jnp.float32)
    vf = v.astype(jnp.float32)
    scale = dh ** -0.5

    def one_block(qi):
        s = jnp.einsum('bhgqd,bhkd->bhgqk', qi, kf) * scale
        p = jax.nn.softmax(s, axis=-1)
        return jnp.einsum('bhgqk,bhkd->bhgqd', p, vf)

    o = lax.map(one_block, qb)
    return o.transpose(1, 2, 3, 0, 4, 5).reshape(B, Hq, S, dh).astype(q.dtype)


def short_conv(x, w):
    K, C = w.shape
    return lax.conv_general_dilated(x, w[:, None, :], window_strides=(1,), padding=[(K // 2, K // 2)],
                                    dimension_numbers=('NWC', 'WIO', 'NWC'), feature_group_count=C)


def gated_delta_chunked(q, k, v, beta, g):
    B, H, S, dk = q.shape
    dv = v.shape[-1]
    N = S // CHUNK
    q = q.reshape(B, H, N, CHUNK, dk)
    k = k.reshape(B, H, N, CHUNK, dk)
    v = v.reshape(B, H, N, CHUNK, dv)
    beta = beta.reshape(B, H, N, CHUNK)
    gc = jnp.cumsum(g.reshape(B, H, N, CHUNK), axis=-1)
    idx = jnp.arange(CHUNK)
    incl = idx[:, None] >= idx[None, :]
    strict = idx[:, None] > idx[None, :]
    decay = jnp.exp(jnp.where(incl, gc[..., :, None] - gc[..., None, :], -jnp.inf))
    kb = k * beta[..., None]
    vb = v * beta[..., None]
    L = jnp.where(strict, jnp.einsum('bhncd,bhnsd->bhncs', kb, k) * decay, 0.0)
    eye = jnp.eye(CHUNK, dtype=jnp.float32)
    T = lax.linalg.triangular_solve(eye + L, jnp.broadcast_to(eye, L.shape), left_side=True, lower=True,
                                    unit_diagonal=True)
    u = T @ vb
    w = T @ (kb * jnp.exp(gc)[..., None])
    a_intra = jnp.where(incl, jnp.einsum('bhncd,bhnsd->bhncs', q, k) * decay, 0.0)
    qg = q * jnp.exp(gc)[..., None]
    kd = k * jnp.exp(gc[..., -1:] - gc)[..., None]
    gl = jnp.exp(gc[..., -1])
    xs = tuple(jnp.moveaxis(t, 2, 0) for t in (u, w, a_intra, qg, kd, gl))

    def step(state, inp):
        u_n, w_n, a_n, qg_n, kd_n, gl_n = inp
        v_new = u_n - w_n @ state
        o_n = qg_n @ state + a_n @ v_new
        state = state * gl_n[..., None, None] + jnp.einsum('bhck,bhcv->bhkv', kd_n, v_new)
        return state, o_n

    _, o = lax.scan(step, jnp.zeros((B, H, dk, dv), jnp.float32), xs)
    return jnp.moveaxis(o, 0, 2).reshape(B, H, S, dv)


def hybrid_mixer(h, row, col, w_in, q_norm_w, k_norm_w, conv_w, a_log, dt_bias, dn_norm_w, w_attn_up, w_dn_up, w_o):
    B, S, D = h.shape
    proj = h @ w_in
    aq, ak, av, dq, dk, dv, dz, b_raw, a_raw, g_raw = jnp.split(proj, np.cumsum(IN_SPLITS)[:-1].tolist(), axis=-1)

    q = axial_rope(rmsnorm(heads(aq, N_Q_HEADS), q_norm_w), row, col)
    k = axial_rope(rmsnorm(heads(ak, N_KV_HEADS), k_norm_w), row, col)
    v = heads(av, N_KV_HEADS)
    attn = block_attention(q, k, v).transpose(0, 2, 1, 3).reshape(B, S, ATTN_WIDTH)

    qkv = jax.nn.silu(short_conv(jnp.concatenate([dq, dk, dv], axis=-1), conv_w))
    cq, ck, cv = jnp.split(qkv, 3, axis=-1)
    qd = l2norm(heads(cq, DN_HEADS)) * (DN_HEAD_DIM ** -0.5)
    kd = l2norm(heads(ck, DN_HEADS))
    vd = heads(cv, DN_HEADS).astype(jnp.float32)
    beta = jax.nn.sigmoid(b_raw.astype(jnp.float32)).reshape(B, S, 2, DN_HEADS).transpose(2, 0, 3, 1)
    a_in = a_raw.astype(jnp.float32).reshape(B, S, 2, DN_HEADS) + dt_bias.astype(jnp.float32)
    g = -jnp.exp(a_log.astype(jnp.float32))[:, None, :, None] * jax.nn.softplus(a_in.transpose(2, 0, 3, 1))
    flip = lambda t: jnp.flip(t, axis=2)
    o_fwd = gated_delta_chunked(qd, kd, vd, beta[0], g[0])
    o_bwd = flip(gated_delta_chunked(flip(qd), flip(kd), flip(vd), flip(beta[1]), flip(g[1])))
    o = (o_fwd + o_bwd).transpose(0, 2, 1, 3)
    o = rmsnorm(o, dn_norm_w) * jax.nn.silu(dz.astype(jnp.float32).reshape(B, S, DN_HEADS, DN_HEAD_DIM))
    dn = o.reshape(B, S, DN_WIDTH).astype(h.dtype)

    gate_a, gate_d = jnp.split(jax.nn.sigmoid(g_raw), N_BRANCHES, axis=-1)
    merged = gate_a * (attn @ w_attn_up) + gate_d * (dn @ w_dn_up)
    return merged @ w_o


def expert_choice_ffn(h, w_router, w_gate, w_up, w_down):
    B, S, D = h.shape
    cap = CAPACITY_FACTOR * S // N_EXPERTS
    aff = jax.nn.softmax((h @ w_router).astype(jnp.float32), axis=-1)
    gates, idx = lax.top_k(aff.transpose(0, 2, 1), cap)
    bidx = jnp.arange(B)[:, None, None]
    xe = h[bidx, idx]
    hg = jnp.einsum('becd,edf->becf', xe, w_gate)
    hu = jnp.einsum('becd,edf->becf', xe, w_up)
    y = jnp.einsum('becf,efd->becd', jax.nn.silu(hg) * hu, w_down)
    y = y * gates[..., None].astype(y.dtype)
    return jnp.zeros_like(h).at[bidx, idx].add(y)


def setup_inputs(seed: int = 0) -> dict:
    key = jax.random.key(seed)
    ks = jax.random.split(key, 24)
    f32 = jnp.float32

    def nrm(k, shape, fan_in):
        return jax.random.normal(k, shape, f32) * (fan_in ** -0.5)

    def gain(k, shape):
        return 1.0 + 0.02 * jax.random.normal(k, shape, f32)

    dt = jnp.exp(jax.random.uniform(ks[11], (DEPTH, 2, DN_HEADS), f32, np.log(1e-3), np.log(1e-1)))
    return {
        'x': jax.random.normal(ks[0], (BATCH, SEQ, D_MODEL), f32),
        'c': jax.random.normal(ks[1], (BATCH, D_MODEL), f32),
        'w_ada': nrm(ks[2], (DEPTH, D_MODEL, 6 * D_MODEL), D_MODEL),
        'b_ada': 0.01 * jax.random.normal(ks[3], (DEPTH, 6 * D_MODEL), f32),
        'norm1_w': gain(ks[4], (DEPTH, D_MODEL)),
        'w_in': nrm(ks[5], (DEPTH, D_MODEL, D_IN), D_MODEL),
        'q_norm_w': gain(ks[6], (DEPTH, HEAD_DIM)),
        'k_norm_w': gain(ks[7], (DEPTH, HEAD_DIM)),
        'conv_w': nrm(ks[8], (DEPTH, CONV_WIDTH, 3 * DN_WIDTH), CONV_WIDTH),
        'a_log': jnp.log(jax.random.uniform(ks[9], (DEPTH, 2, DN_HEADS), f32, 1.0, 16.0)),
        'dt_bias': dt + jnp.log(-jnp.expm1(-dt)),
        'dn_norm_w': gain(ks[10], (DEPTH, DN_HEAD_DIM)),
        'w_attn_up': nrm(ks[12], (DEPTH, ATTN_WIDTH, D_MODEL), ATTN_WIDTH),
        'w_dn_up': nrm(ks[13], (DEPTH, DN_WIDTH, D_MODEL), DN_WIDTH),
        'w_o': nrm(ks[14], (DEPTH, D_MODEL, D_MODEL), D_MODEL),
        'norm2_w': gain(ks[15], (DEPTH, D_MODEL)),
        'w_router': nrm(ks[16], (DEPTH, D_MODEL, N_EXPERTS), D_MODEL),
        'w_gate': nrm(ks[17], (DEPTH, N_EXPERTS, D_MODEL, D_EXPERT), D_MODEL),
        'w_up': nrm(ks[18], (DEPTH, N_EXPERTS, D_MODEL, D_EXPERT), D_MODEL),
        'w_down': nrm(ks[19], (DEPTH, N_EXPERTS, D_EXPERT, D_MODEL), D_EXPERT),
    }


def reference(x, c, w_ada, b_ada, norm1_w, w_in, q_norm_w, k_norm_w, conv_w, a_log, dt_bias, dn_norm_w,
              w_attn_up, w_dn_up, w_o, norm2_w, w_router, w_gate, w_up, w_down):
    B, S, D = x.shape
    rows = S // GRID_W
    row = jnp.broadcast_to(jnp.arange(rows)[:, None], (rows, GRID_W)).reshape(S)
    col = jnp.broadcast_to(jnp.arange(GRID_W)[None, :], (rows, GRID_W)).reshape(S)
    for l in range(DEPTH):
        mod = jax.nn.silu(c) @ w_ada[l] + b_ada[l]
        sh1, sc1, gt1, sh2, sc2, gt2 = [m[:, None, :] for m in jnp.split(mod, 6, axis=-1)]
        h = rmsnorm(x, norm1_w[l]) * (1.0 + sc1) + sh1
        x = x + gt1 * hybrid_mixer(h, row, col, w_in[l], q_norm_w[l], k_norm_w[l], conv_w[l], a_log[l], dt_bias[l],
                                   dn_norm_w[l], w_attn_up[l], w_dn_up[l], w_o[l])
        h2 = rmsnorm(x, norm2_w[l]) * (1.0 + sc2) + sh2
        x = x + gt2 * expert_choice_ffn(h2, w_router[l], w_gate[l], w_up[l], w_down[l])
    return x
```

```python
import functools

import numpy as np
import jax
import jax.numpy as jnp
from jax import lax
from jax.experimental import pallas as pl
from jax.experimental.pallas import tpu as pltpu

F32 = jnp.float32
BF16 = jnp.bfloat16

GRID_W = 64
N_Q_HEADS = 8
N_KV_HEADS = 2
HEAD_DIM = 64
ATTN_WIDTH = N_Q_HEADS * HEAD_DIM
KV_WIDTH = N_KV_HEADS * HEAD_DIM
ROPE_THETA = 10000.0
DN_HEADS = 8
DN_HEAD_DIM = 64
DN_WIDTH = DN_HEADS * DN_HEAD_DIM
CONV_WIDTH = 5
CHUNK = 64
N_EXPERTS = 16
CAPACITY_FACTOR = 2
EPS = 1e-6
LANES = 128
VMEM_LIMIT = 56 * 1024 * 1024

NT_DIMS = (((1,), (1,)), ((), ()))


def _params(sem):
    return pltpu.CompilerParams(dimension_semantics=sem, vmem_limit_bytes=VMEM_LIMIT)


def _dot(a, b):
    return jnp.dot(a, b, preferred_element_type=F32)


def _split_dot(x, m, parts):
    acc = None
    r = x
    for i in range(parts):
        h = r.astype(BF16)
        d = _dot(h, m)
        acc = d if acc is None else acc + d
        if i + 1 < parts:
            r = r - h.astype(F32)
    return acc


def _split_dot_l(m, x, parts):
    acc = None
    r = x
    for i in range(parts):
        h = r.astype(BF16)
        d = _dot(m, h)
        acc = d if acc is None else acc + d
        if i + 1 < parts:
            r = r - h.astype(F32)
    return acc


def _silu(x):
    return x * jax.nn.sigmoid(x)


def _ada_kernel(c_ref, w_ref, b_ref, o_ref):
    c = c_ref[...]
    o_ref[...] = jnp.dot(_silu(c), w_ref[...], preferred_element_type=F32,
                         precision=lax.Precision.HIGHEST) + b_ref[...]


def _ada(c, w, b):
    bsz, d = c.shape
    n = w.shape[1]
    tn = 1536
    rows = 8
    c8 = jnp.zeros((rows, d), F32).at[:bsz].set(c)
    out = pl.pallas_call(
        _ada_kernel,
        grid=(n // tn,),
        in_specs=[pl.BlockSpec((rows, d), lambda j: (0, 0)),
                  pl.BlockSpec((d, tn), lambda j: (0, j)),
                  pl.BlockSpec((1, tn), lambda j: (0, j))],
        out_specs=pl.BlockSpec((rows, tn), lambda j: (0, j)),
        out_shape=jax.ShapeDtypeStruct((rows, n), F32),
        compiler_params=_params(("arbitrary",)),
        name="ada",
    )(c8, w, b.reshape(1, n))
    return out[:bsz]


def _inproj_kernel(x_ref, nw_ref, sh_ref, sc_ref, wa_ref, wd_ref, wz_ref, wb_ref, wal_ref, wg_ref,
                   oa_ref, od_ref, oz_ref, ob_ref, oal_ref, og_ref):
    x = x_ref[0]
    ms = jnp.mean(x * x, axis=-1, keepdims=True)
    h = x * lax.rsqrt(ms + EPS) * nw_ref[...]
    h = h * (1.0 + sc_ref[0]) + sh_ref[0]
    hb = h.astype(BF16)
    oa_ref[0] = _dot(hb, wa_ref[...]).astype(BF16)
    od_ref[0] = _dot(hb, wd_ref[...]).astype(BF16)
    oz_ref[0] = _dot(hb, wz_ref[...]).astype(BF16)
    ob_ref[0] = _dot(hb, wb_ref[...])
    oal_ref[0] = _dot(hb, wal_ref[...])
    og_ref[0] = _dot(hb, wg_ref[...]).astype(BF16)


def _inproj(x, nw, sh, sc, ws, tm):
    bsz, s, d = x.shape
    widths = [w.shape[1] for w in ws]
    dts = [BF16, BF16, BF16, F32, F32, BF16]
    tok = lambda w: pl.BlockSpec((1, tm, w), lambda b, i: (b, i, 0))
    full = lambda a: pl.BlockSpec(a.shape, lambda b, i: (0,) * a.ndim)
    vec = pl.BlockSpec((1, 1, d), lambda b, i: (b, 0, 0))
    return pl.pallas_call(
        _inproj_kernel,
        grid=(bsz, s // tm),
        in_specs=[tok(d), full(nw), vec, vec] + [full(w) for w in ws],
        out_specs=[tok(w) for w in widths],
        out_shape=[jax.ShapeDtypeStruct((bsz, s, w), dt) for w, dt in zip(widths, dts)],
        compiler_params=_params(("parallel", "parallel")),
        name="inproj",
    )(x, nw, sh, sc, *ws)


def _norm_rope(x, w, bd, cos, sa, sb):
    n = x.shape[1]
    ms = _split_dot(x * x, bd, 2)
    xn = x * lax.rsqrt(ms + EPS) * w
    return xn * cos + pltpu.roll(xn, n - 16, 1) * sa + pltpu.roll(xn, 16, 1) * sb


def _attn_prep_kernel(a_ref, cos_ref, sa_ref, sb_ref, qw_ref, kw_ref, bdq_ref, bdk_ref,
                      q_ref, kt_ref, v_ref):
    a = a_ref[0].astype(F32)
    q = a[:, :ATTN_WIDTH]
    k = a[:, ATTN_WIDTH:ATTN_WIDTH + KV_WIDTH]
    v = a[:, ATTN_WIDTH + KV_WIDTH:]
    cos, sa, sb = cos_ref[...], sa_ref[...], sb_ref[...]
    rep = ATTN_WIDTH // LANES
    tile = lambda t: jnp.concatenate([t] * rep, axis=1)
    qr = _norm_rope(q, qw_ref[...], bdq_ref[...], tile(cos), tile(sa), tile(sb))
    q_ref[0] = (qr * (HEAD_DIM ** -0.5)).astype(BF16)
    kr = _norm_rope(k, kw_ref[...], bdk_ref[...], cos, sa, sb)
    kt = kr.T
    kt_ref[0, 0] = kt[:HEAD_DIM].astype(BF16)
    kt_ref[0, 1] = kt[HEAD_DIM:].astype(BF16)
    lane = lax.broadcasted_iota(jnp.int32, v.shape, 1)
    ones_col = jnp.where(lane == HEAD_DIM, 1.0, 0.0)
    v_ref[0, 0] = jnp.where(lane < HEAD_DIM, v, ones_col).astype(BF16)
    v_ref[0, 1] = jnp.where(lane < HEAD_DIM, pltpu.roll(v, HEAD_DIM, 1), ones_col).astype(BF16)


def _rope_tables(s):
    pos = jnp.arange(s)
    lane = jnp.arange(LANES)
    d = lane % HEAD_DIM
    p = d % 32
    f = (p % 16).astype(F32)
    freqs = ROPE_THETA ** (-(f * 2.0 / 32.0))
    axis_pos = jnp.where((d // 32)[None, :] == 0, (pos // GRID_W)[:, None], (pos % GRID_W)[:, None])
    ang = axis_pos.astype(F32) * freqs[None, :]
    cos, sin = jnp.cos(ang), jnp.sin(ang)
    first = (p < 16)[None, :]
    return cos, jnp.where(first, -sin, 0.0), jnp.where(first, 0.0, sin)


def _block_diag(n, blk, val):
    i = np.arange(n)
    return jnp.asarray(np.where((i[:, None] // blk) == (i[None, :] // blk), val, 0.0), BF16)


def _attn_prep(a, qw, kw, tp):
    bsz, s, wa = a.shape
    cos, sa, sb = _rope_tables(s)
    bdq = _block_diag(ATTN_WIDTH, HEAD_DIM, 1.0 / HEAD_DIM)
    bdk = _block_diag(KV_WIDTH, HEAD_DIM, 1.0 / HEAD_DIM)
    qw_t = jnp.tile(qw, N_Q_HEADS).reshape(1, ATTN_WIDTH)
    kw_t = jnp.tile(kw, N_KV_HEADS).reshape(1, KV_WIDTH)
    tab = pl.BlockSpec((tp, LANES), lambda b, i: (i, 0))
    full = lambda t: pl.BlockSpec(t.shape, lambda b, i: (0,) * t.ndim)
    return pl.pallas_call(
        _attn_prep_kernel,
        grid=(bsz, s // tp),
        in_specs=[pl.BlockSpec((1, tp, wa), lambda b, i: (b, i, 0)), tab, tab, tab,
                  full(qw_t), full(kw_t), full(bdq), full(bdk)],
        out_specs=[pl.BlockSpec((1, tp, ATTN_WIDTH), lambda b, i: (b, i, 0)),
                   pl.BlockSpec((1, N_KV_HEADS, HEAD_DIM, tp), lambda b, i: (b, 0, 0, i)),
                   pl.BlockSpec((1, N_KV_HEADS, tp, LANES), lambda b, i: (b, 0, i, 0))],
        out_shape=[jax.ShapeDtypeStruct((bsz, s, ATTN_WIDTH), BF16),
                   jax.ShapeDtypeStruct((bsz, N_KV_HEADS, HEAD_DIM, s), BF16),
                   jax.ShapeDtypeStruct((bsz, N_KV_HEADS, s, LANES), BF16)],
        compiler_params=_params(("parallel", "parallel")),
        name="attn_prep",
    )(a, cos, sa, sb, qw_t, kw_t, bdq, bdk)


def _attn_kernel(q_ref, kt_ref, v_ref, o_ref, qs, m_s, acc, *, tq, group):
    ki = pl.program_id(3)

    @pl.when(ki == 0)
    def _():
        q = q_ref[0]
        for h in range(group):
            qs[h * tq:(h + 1) * tq, :] = q[:, h * HEAD_DIM:(h + 1) * HEAD_DIM]
        m_s[...] = jnp.full(m_s.shape, -jnp.inf, F32)
        acc[...] = jnp.zeros(acc.shape, F32)

    s = _dot(qs[...], kt_ref[0, 0])
    m_old = m_s[...]
    m_new = jnp.maximum(m_old, jnp.max(s, axis=1, keepdims=True))
    alpha = jnp.exp(m_old - m_new)
    p = jnp.exp(s - m_new)
    acc[...] = alpha * acc[...] + _dot(p.astype(BF16), v_ref[0, 0])
    m_s[...] = m_new

    @pl.when(ki == pl.num_programs(3) - 1)
    def _():
        a = acc[...]
        o = a[:, :HEAD_DIM] * (1.0 / a[:, HEAD_DIM:HEAD_DIM + 1])
        o_ref[0] = jnp.concatenate([o[h * tq:(h + 1) * tq] for h in range(group)], axis=1).astype(BF16)


def _attention(q, kt, v, tq, tk):
    bsz, s, _ = q.shape
    group = N_Q_HEADS // N_KV_HEADS
    gw = group * HEAD_DIM
    return pl.pallas_call(
        functools.partial(_attn_kernel, tq=tq, group=group),
        grid=(bsz, N_KV_HEADS, s // tq, s // tk),
        in_specs=[pl.BlockSpec((1, tq, gw), lambda b, g, i, j: (b, i, g)),
                  pl.BlockSpec((1, 1, HEAD_DIM, tk), lambda b, g, i, j: (b, g, 0, j)),
                  pl.BlockSpec((1, 1, tk, LANES), lambda b, g, i, j: (b, g, j, 0))],
        out_specs=pl.BlockSpec((1, tq, gw), lambda b, g, i, j: (b, i, g)),
        out_shape=jax.ShapeDtypeStruct((bsz, s, ATTN_WIDTH), BF16),
        scratch_shapes=[pltpu.VMEM((group * tq, HEAD_DIM), BF16),
                        pltpu.VMEM((group * tq, 1), F32),
                        pltpu.VMEM((group * tq, LANES), F32)],
        compiler_params=_params(("parallel", "parallel", "parallel", "arbitrary")),
        name="attention",
    )(q, kt, v)


def _dn_prep_kernel(cur_ref, prev_ref, next_ref, cw_ref, braw_ref, araw_ref, alog_ref, dtb_ref,
                    bd_ref, trif_ref, trib_ref, tot_ref, sel_ref, exf_ref, exb_ref,
                    kn_ref, qn_ref, vb_ref, win_ref, qg_ref, kdt_ref, gc_ref, gct_ref, beta_ref, gl_ref,
                    ext, *, ts):
    i = pl.program_id(1)
    halo = 16
    w3 = 3 * DN_WIDTH
    ext[halo:halo + ts, :] = cur_ref[0].astype(F32)
    ext[0:halo, :] = jnp.where(i > 0, prev_ref[0].astype(F32), 0.0)
    ext[halo + ts:, :] = jnp.where(i < pl.num_programs(1) - 1, next_ref[0].astype(F32), 0.0)
    cw = cw_ref[...]
    conv = jnp.zeros((ts, w3), F32)
    for j in range(CONV_WIDTH):
        off = halo - CONV_WIDTH // 2 + j
        conv = conv + ext[off:off + ts, :] * cw[j:j + 1, :]
    act = _silu(conv)
    cq, ck, cv = act[:, :DN_WIDTH], act[:, DN_WIDTH:2 * DN_WIDTH], act[:, 2 * DN_WIDTH:]
    bd = bd_ref[...]
    qn = cq * lax.rsqrt(_split_dot(cq * cq, bd, 2) + EPS) * (DN_HEAD_DIM ** -0.5)
    kn = ck * lax.rsqrt(_split_dot(ck * ck, bd, 2) + EPS)
    kn_ref[0] = kn.astype(BF16)
    qn_ref[0] = qn.astype(BF16)

    beta = jax.nn.sigmoid(braw_ref[0])
    g = -jnp.exp(alog_ref[...]) * jax.nn.softplus(araw_ref[0] + dtb_ref[...])
    gtot = _split_dot_l(tot_ref[...], g, 3)
    gl = jnp.exp(_split_dot_l(sel_ref[...], g, 3))
    for d, (tri_ref, ex_ref) in enumerate(((trif_ref, exf_ref), (trib_ref, exb_ref))):
        gc = _split_dot_l(tri_ref[...], g, 3)
        ex = ex_ref[...]
        beta_x = _split_dot(beta, ex, 2)
        eg_x = _split_dot(jnp.exp(gc), ex, 2)
        ek_x = _split_dot(jnp.exp(gtot - gc), ex, 2)
        vb_ref[d, 0] = (cv * beta_x).astype(BF16)
        win_ref[d, 0] = (kn * beta_x * eg_x).astype(BF16)
        qg_ref[d, 0] = (qn * eg_x).astype(BF16)
        kdt_ref[d, 0] = (kn * ek_x).T.astype(BF16)
        shift = (LANES - d * DN_HEADS) % LANES
        gc_d = gc if d == 0 else pltpu.roll(gc, shift, 1)
        gc_ref[d, 0] = gc_d
        gct_ref[d, 0] = gc_d.T[:DN_HEADS]
        beta_ref[d, 0] = beta if d == 0 else pltpu.roll(beta, shift, 1)
        gl_ref[d, 0] = _split_dot(gl, ex, 2)


def _dn_prep(dqkv, braw, araw, conv_w, a_log, dt_bias, ts):
    bsz, s, w3 = dqkv.shape
    nt = s // ts
    cpt = ts // CHUNK
    assert cpt == 8
    halo = 16
    hb = ts // halo
    idx = np.arange(ts)
    same = (idx[:, None] // CHUNK) == (idx[None, :] // CHUNK)
    trif = jnp.asarray(np.where(same & (idx[:, None] >= idx[None, :]), 1.0, 0.0), BF16)
    trib = jnp.asarray(np.where(same & (idx[:, None] <= idx[None, :]), 1.0, 0.0), BF16)
    tot = jnp.asarray(np.where(same, 1.0, 0.0), BF16)
    sel = jnp.asarray(np.where(np.arange(cpt)[:, None] == (idx[None, :] // CHUNK), 1.0, 0.0), BF16)
    bd = _block_diag(DN_WIDTH, DN_HEAD_DIM, 1.0)
    lane = np.arange(DN_WIDTH) // DN_HEAD_DIM
    row = np.arange(LANES)
    exf = jnp.asarray(np.where(row[:, None] == lane[None, :], 1.0, 0.0), BF16)
    exb = jnp.asarray(np.where(row[:, None] == lane[None, :] + DN_HEADS, 1.0, 0.0), BF16)
    nh2 = 2 * DN_HEADS
    alog = jnp.zeros((1, LANES), F32).at[0, :nh2].set(a_log.reshape(nh2))
    dtb = jnp.zeros((1, LANES), F32).at[0, :nh2].set(dt_bias.reshape(nh2))
    full = lambda t: pl.BlockSpec(t.shape, lambda b, i: (0,) * t.ndim)
    tok = lambda w: pl.BlockSpec((1, ts, w), lambda b, i: (b, i, 0))
    dtok = lambda w: pl.BlockSpec((2, 1, ts, w), lambda b, i: (0, b, i, 0))
    sds = jax.ShapeDtypeStruct
    return pl.pallas_call(
        functools.partial(_dn_prep_kernel, ts=ts),
        grid=(bsz, nt),
        in_specs=[tok(w3),
                  pl.BlockSpec((1, halo, w3), lambda b, i: (b, jnp.maximum(i * hb - 1, 0), 0)),
                  pl.BlockSpec((1, halo, w3), lambda b, i: (b, jnp.minimum((i + 1) * hb, s // halo - 1), 0)),
                  full(conv_w), tok(LANES), tok(LANES), full(alog), full(dtb),
                  full(bd), full(trif), full(trib), full(tot), full(sel), full(exf), full(exb)],
        out_specs=[tok(DN_WIDTH), tok(DN_WIDTH), dtok(DN_WIDTH), dtok(DN_WIDTH), dtok(DN_WIDTH),
                   pl.BlockSpec((2, 1, DN_WIDTH, ts), lambda b, i: (0, b, 0, i)),
                   dtok(LANES),
                   pl.BlockSpec((2, 1, DN_HEADS, ts), lambda b, i: (0, b, 0, i)),
                   dtok(LANES),
                   pl.BlockSpec((2, 1, cpt, DN_WIDTH), lambda b, i: (0, b, i, 0))],
        out_shape=[sds((bsz, s, DN_WIDTH), BF16), sds((bsz, s, DN_WIDTH), BF16),
                   sds((2, bsz, s, DN_WIDTH), BF16), sds((2, bsz, s, DN_WIDTH), BF16),
                   sds((2, bsz, s, DN_WIDTH), BF16), sds((2, bsz, DN_WIDTH, s), BF16),
                   sds((2, bsz, s, LANES), F32), sds((2, bsz, DN_HEADS, s), F32),
                   sds((2, bsz, s, LANES), F32), sds((2, bsz, s // CHUNK, DN_WIDTH), F32)],
        scratch_shapes=[pltpu.VMEM((ts + 2 * halo, w3), F32)],
        compiler_params=_params(("parallel", "parallel")),
        name="dn_prep",
    )(dqkv, dqkv, dqkv, conv_w, braw, araw, alog, dtb, bd, trif, trib, tot, sel, exf, exb)


def _dn_scan_kernel(kn_ref, qn_ref, vb_ref, win_ref, qg_ref, kdt_ref, gc_ref, gct_ref, beta_ref, gl_ref,
                    o_ref, state, *, rev, cpb, gl_rows):
    j = pl.program_id(1)
    nb = pl.num_programs(1)

    @pl.when(j == 0)
    def _():
        state[...] = jnp.zeros(state.shape, F32)

    jj = (nb - 1 - j) if rev else j
    ri = lax.broadcasted_iota(jnp.int32, (CHUNK, CHUNK), 0)
    ci = lax.broadcasted_iota(jnp.int32, (CHUNK, CHUNK), 1)
    incl = (ri <= ci) if rev else (ri >= ci)
    strict = (ri < ci) if rev else (ri > ci)
    eye = jnp.where(ri == ci, 1.0, 0.0)
    merge = [((ri // (2 * sz)) == (ci // (2 * sz))) & ((ri // sz) != (ci // sz))
             for sz in (1, 2, 4, 8, 16, 32)]

    order = range(cpb - 1, -1, -1) if rev else range(cpb)
    for c in order:
        rows = slice(c * CHUNK, (c + 1) * CHUNK)
        k_c, q_c = kn_ref[0, rows, :], qn_ref[0, rows, :]
        vb_c, win_c, qg_c = vb_ref[0, 0, rows, :], win_ref[0, 0, rows, :], qg_ref[0, 0, rows, :]
        gc_c, beta_c = gc_ref[0, 0, rows, :], beta_ref[0, 0, rows, :]
        gct_c = gct_ref[0, 0, :, rows]
        gl_c = gl_ref[0, 0, pl.ds((jj * cpb) % gl_rows + c, 1), :]
        outs = []
        for h in range(DN_HEADS):
            ls = slice(h * DN_HEAD_DIM, (h + 1) * DN_HEAD_DIM)
            k_h, q_h = k_c[:, ls], q_c[:, ls]
            a_kk = lax.dot_general(k_h, k_h, NT_DIMS, preferred_element_type=F32)
            a_qk = lax.dot_general(q_h, k_h, NT_DIMS, preferred_element_type=F32)
            diff = gc_c[:, h:h + 1] - gct_c[h:h + 1, :]
            dec = jnp.exp(jnp.where(incl, diff, -jnp.inf))
            lm = jnp.where(strict, a_kk * beta_c[:, h:h + 1] * dec, 0.0)
            t = eye - jnp.where(merge[0], lm, 0.0)
            for mk in merge[1:]:
                cb = jnp.where(mk, lm, 0.0).astype(BF16)
                tb = t.astype(BF16)
                t = t - _dot(tb, _dot(cb, tb).astype(BF16))
            tb = t.astype(BF16)
            u = _dot(tb, vb_c[:, ls])
            w = _dot(tb, win_c[:, ls])
            s_h = state[h]
            sb = s_h.astype(BF16)
            v_new = u - _dot(w.astype(BF16), sb)
            vnb = v_new.astype(BF16)
            outs.append(_dot(qg_c[:, ls], sb) + _dot((a_qk * dec).astype(BF16), vnb))
            state[h] = s_h * gl_c[:, ls] + _dot(kdt_ref[0, 0, ls, rows], vnb)
        o_ref[0, rows, :] = jnp.concatenate(outs, axis=1)


def _dn_scan(prep, rev, tsb, gl_rows):
    kn, qn, vb, win, qg, kdt, gc, gct, beta, gl = prep
    bsz, s, _ = kn.shape
    nb = s // tsb
    cpb = tsb // CHUNK
    d = 1 if rev else 0
    blk = (lambda j: nb - 1 - j) if rev else (lambda j: j)
    tok = pl.BlockSpec((1, tsb, DN_WIDTH), lambda b, j: (b, blk(j), 0))
    dtok = lambda w: pl.BlockSpec((1, 1, tsb, w), lambda b, j: (d, b, blk(j), 0))
    return pl.pallas_call(
        functools.partial(_dn_scan_kernel, rev=rev, cpb=cpb, gl_rows=gl_rows),
        grid=(bsz, nb),
        in_specs=[tok, tok, dtok(DN_WIDTH), dtok(DN_WIDTH), dtok(DN_WIDTH),
                  pl.BlockSpec((1, 1, DN_WIDTH, tsb), lambda b, j: (d, b, 0, blk(j))),
                  dtok(LANES),
                  pl.BlockSpec((1, 1, DN_HEADS, tsb), lambda b, j: (d, b, 0, blk(j))),
                  dtok(LANES),
                  pl.BlockSpec((1, 1, gl_rows, DN_WIDTH), lambda b, j: (d, b, (blk(j) * cpb) // gl_rows, 0))],
        out_specs=tok,
        out_shape=jax.ShapeDtypeStruct((bsz, s, DN_WIDTH), F32),
        scratch_shapes=[pltpu.VMEM((DN_HEADS, DN_HEAD_DIM, DN_HEAD_DIM), F32)],
        compiler_params=_params(("parallel", "arbitrary")),
        name="dn_scan_bwd" if rev else "dn_scan_fwd",
    )(kn, qn, vb, win, qg, kdt, gc, gct, beta, gl)


def _post_kernel(attn_ref, of_ref, ob_ref, dz_ref, g_ref, x_ref, gt1_ref, sh2_ref, sc2_ref,
                 dnw_ref, n2w_ref, bd_ref, wau_ref, wdu_ref, wo_ref, wr_ref,
                 x1_ref, h2_ref, aff_ref, afft_ref):
    o = of_ref[0] + ob_ref[0]
    ms = _split_dot(o * o, bd_ref[...], 2)
    dn = o * lax.rsqrt(ms + EPS) * dnw_ref[...] * _silu(dz_ref[0].astype(F32))
    gates = jax.nn.sigmoid(g_ref[0].astype(F32))
    d = x_ref.shape[2]
    merged = (gates[:, :d] * _dot(attn_ref[0], wau_ref[...])
              + gates[:, d:] * _dot(dn.astype(BF16), wdu_ref[...]))
    x1 = x_ref[0] + gt1_ref[0] * _dot(merged.astype(BF16), wo_ref[...])
    x1_ref[0] = x1
    ms2 = jnp.mean(x1 * x1, axis=-1, keepdims=True)
    h2 = x1 * lax.rsqrt(ms2 + EPS) * n2w_ref[...]
    h2 = h2 * (1.0 + sc2_ref[0]) + sh2_ref[0]
    h2_ref[0] = h2.astype(BF16)
    logits = jnp.dot(h2, wr_ref[...], preferred_element_type=F32, precision=lax.Precision.HIGHEST)
    lane = lax.broadcasted_iota(jnp.int32, logits.shape, 1)
    logits = jnp.where(lane < N_EXPERTS, logits, -jnp.inf)
    e = jnp.exp(logits - jnp.max(logits, axis=1, keepdims=True))
    aff = e / jnp.sum(e, axis=1, keepdims=True)
    aff_ref[0] = aff[:, :N_EXPERTS]
    afft_ref[0] = aff.T[:N_EXPERTS]


def _post(attn, o_f, o_b, dz, graw, x, gt1, sh2, sc2, dnw, n2w, wau, wdu, wo, wr, tm):
    bsz, s, d = x.shape
    bd = _block_diag(DN_WIDTH, DN_HEAD_DIM, 1.0 / DN_HEAD_DIM)
    dnw_t = jnp.tile(dnw, DN_HEADS).reshape(1, DN_WIDTH)
    wr_p = jnp.zeros((d, LANES), F32).at[:, :N_EXPERTS].set(wr)
    tok = lambda w: pl.BlockSpec((1, tm, w), lambda b, i: (b, i, 0))
    full = lambda t: pl.BlockSpec(t.shape, lambda b, i: (0,) * t.ndim)
    vec = pl.BlockSpec((1, 1, d), lambda b, i: (b, 0, 0))
    sds = jax.ShapeDtypeStruct
    return pl.pallas_call(
        _post_kernel,
        grid=(bsz, s // tm),
        in_specs=[tok(ATTN_WIDTH), tok(DN_WIDTH), tok(DN_WIDTH), tok(DN_WIDTH), tok(2 * d), tok(d),
                  vec, vec, vec, full(dnw_t), full(n2w), full(bd), full(wau), full(wdu), full(wo), full(wr_p)],
        out_specs=[tok(d), tok(d), tok(N_EXPERTS),
                   pl.BlockSpec((1, N_EXPERTS, tm), lambda b, i: (b, 0, i))],
        out_shape=[sds((bsz, s, d), F32), sds((bsz, s, d), BF16), sds((bsz, s, N_EXPERTS), F32),
                   sds((bsz, N_EXPERTS, s), F32)],
        compiler_params=_params(("parallel", "parallel")),
        name="post_mixer",
    )(attn, o_f, o_b, dz, graw, x, gt1, sh2, sc2, dnw_t, n2w, bd, wau, wdu, wo, wr_p)


def _topk_kernel(aff_ref, tri_ref, code_ref, *, cap, s):
    a = aff_ref[0]
    ne = a.shape[0]
    count = lambda m: jnp.sum(jnp.where(m, 1.0, 0.0), axis=1, keepdims=True)

    def vbody(i, lo):
        cand = lo | (jnp.int32(1) << (30 - i))
        return jnp.where(count(a >= pltpu.bitcast(cand, F32)) >= cap, cand, lo)

    lo_bits = lax.fori_loop(0, 31, vbody, jnp.zeros((ne, 1), jnp.int32))
    lo = pltpu.bitcast(lo_bits, F32)
    hi = pltpu.bitcast(lo_bits + 1, F32)

    def rbody(i, lh):
        lo, hi = lh
        mid = 0.5 * (lo + hi)
        ok = count(a >= mid) >= cap
        return jnp.where(ok, mid, lo), jnp.where(ok, hi, mid)

    lo, hi = lax.fori_loop(0, 32, rbody, (lo, hi))
    gt = a >= hi
    eq = (a >= lo) & jnp.logical_not(gt)
    need = cap - count(gt)
    idx = lax.broadcasted_iota(jnp.int32, a.shape, 1)
    nbits = int(np.log2(s))

    def ibody(i, x):
        cand = x | (jnp.int32(1) << (nbits - 1 - i))
        return jnp.where(count(eq & (idx < cand)) < need, cand, x)

    last = lax.fori_loop(0, nbits, ibody, jnp.zeros((ne, 1), jnp.int32))
    sel = gt | (eq & (idx <= last))
    self32 = jnp.where(sel, 1.0, 0.0)
    tri = tri_ref[...]
    carry = jnp.zeros((ne, 1), F32)
    for t in range(s // LANES):
        seg = self32[:, t * LANES:(t + 1) * LANES]
        inc = _dot(seg.astype(BF16), tri)
        pos = (inc - seg + carry).astype(jnp.int32)
        code_ref[0, :, t * LANES:(t + 1) * LANES] = pos * 2 + seg.astype(jnp.int32)
        carry = carry + inc[:, LANES - 1:LANES]


def _topk(afft, cap):
    bsz, ne, s = afft.shape
    i = np.arange(LANES)
    tri = jnp.asarray(np.where(i[:, None] <= i[None, :], 1.0, 0.0), BF16)
    return pl.pallas_call(
        functools.partial(_topk_kernel, cap=cap, s=s),
        grid=(bsz,),
        in_specs=[pl.BlockSpec((1, ne, s), lambda b: (b, 0, 0)),
                  pl.BlockSpec((LANES, LANES), lambda b: (0, 0))],
        out_specs=pl.BlockSpec((1, ne, s), lambda b: (b, 0, 0)),
        out_shape=jax.ShapeDtypeStruct((bsz, ne, s), jnp.int32),
        compiler_params=_params(("parallel",)),
        name="topk",
    )(afft, tri)


def _moe_ffn_kernel(starts_ref, code_ref, h2_ref, wg_ref, wu_ref, wd_ref, y_ref, xe, *, tb, win, cap, nj):
    b, e, j = pl.program_id(0), pl.program_id(1), pl.program_id(2)

    @pl.when(j == 0)
    def _():
        xe[...] = jnp.zeros(xe.shape, F32)

    a = pl.multiple_of(starts_ref[(b * pl.num_programs(1) + e) * nj + j], 16)
    code = code_ref[0, 0, 0]
    r = lax.broadcasted_iota(jnp.int32, (win, tb), 0) + a
    onehot = jnp.where((code == 2 * r + 1), 1.0, 0.0).astype(BF16)
    xe[pl.ds(a, win), :] = xe[pl.ds(a, win), :] + _dot(onehot, h2_ref[0])

    @pl.when(j == nj - 1)
    def _():
        xb = xe[0:cap, :].astype(BF16)
        hg = _dot(xb, wg_ref[0, 0].astype(BF16))
        hu = _dot(xb, wu_ref[0, 0].astype(BF16))
        act = (_silu(hg) * hu).astype(BF16)
        y_ref[0, 0, 0:cap, :] = _dot(act, wd_ref[0, 0].astype(BF16)).astype(BF16)
        y_ref[0, 0, cap:, :] = jnp.zeros((y_ref.shape[2] - cap, y_ref.shape[3]), BF16)


def _moe_ffn(starts, code, h2, w_gate, w_up, w_down, cap, tb, win):
    bsz, s, d = h2.shape
    ne = code.shape[1]
    nj = s // tb
    f = w_gate.shape[-1]
    capp = cap + win
    code5 = code.reshape(bsz, ne, nj, 1, tb)
    wspec = lambda r, c: pl.BlockSpec((1, 1, r, c), lambda b, e, j, st: (0, e, 0, 0))
    return pl.pallas_call(
        functools.partial(_moe_ffn_kernel, tb=tb, win=win, cap=cap, nj=nj),
        grid_spec=pltpu.PrefetchScalarGridSpec(
            num_scalar_prefetch=1,
            grid=(bsz, ne, nj),
            in_specs=[pl.BlockSpec((1, 1, 1, 1, tb), lambda b, e, j, st: (b, e, j, 0, 0)),
                      pl.BlockSpec((1, tb, d), lambda b, e, j, st: (b, j, 0)),
                      wspec(d, f), wspec(d, f), wspec(f, d)],
            out_specs=pl.BlockSpec((1, 1, capp, d), lambda b, e, j, st: (b, e, 0, 0)),
            scratch_shapes=[pltpu.VMEM((capp, d), F32)]),
        out_shape=jax.ShapeDtypeStruct((bsz, ne, capp, d), BF16),
        compiler_params=_params(("parallel", "parallel", "arbitrary")),
        name="moe_ffn",
    )(starts, code5, h2, w_gate, w_up, w_down)


def _moe_scatter_kernel(starts_ref, codet_ref, affc_ref, y_ref, x1_ref, gt2_ref, o_ref, acc, *, tb, win, nj):
    b, j, e = pl.program_id(0), pl.program_id(1), pl.program_id(2)
    ne = pl.num_programs(2)

    @pl.when(e == 0)
    def _():
        acc[...] = jnp.zeros(acc.shape, F32)

    a = starts_ref[(b * ne + e) * nj + j]
    lane = lax.broadcasted_iota(jnp.int32, codet_ref.shape[1:], 1)
    pick = lane == e
    code = jnp.sum(jnp.where(pick, codet_ref[0].astype(F32), 0.0), axis=1, keepdims=True).astype(jnp.int32)
    gate = jnp.sum(jnp.where(pick, affc_ref[0], 0.0), axis=1, keepdims=True)
    r = lax.broadcasted_iota(jnp.int32, (tb, win), 1) + a
    onehot = jnp.where(code == 2 * r + 1, 1.0, 0.0).astype(BF16)
    acc[...] = acc[...] + gate * _dot(onehot, y_ref[...])

    @pl.when(e == ne - 1)
    def _():
        o_ref[0] = x1_ref[0] + gt2_ref[0] * acc[...]


def _moe_scatter(starts, codet, affc, y, x1, gt2, tb, win):
    bsz, s, d = x1.shape
    ne = codet.shape[2]
    nj = s // tb
    return pl.pallas_call(
        functools.partial(_moe_scatter_kernel, tb=tb, win=win, nj=nj),
        grid_spec=pltpu.PrefetchScalarGridSpec(
            num_scalar_prefetch=1,
            grid=(bsz, nj, ne),
            in_specs=[pl.BlockSpec((1, tb, ne), lambda b, j, e, st: (b, j, 0)),
                      pl.BlockSpec((1, tb, ne), lambda b, j, e, st: (b, j, 0)),
                      pl.BlockSpec((pl.squeezed, pl.squeezed, pl.Element(win), pl.Element(d)),
                                   lambda b, j, e, st: (b, e, pl.multiple_of(st[(b * ne + e) * nj + j], 16), 0)),
                      pl.BlockSpec((1, tb, d), lambda b, j, e, st: (b, j, 0)),
                      pl.BlockSpec((1, 1, d), lambda b, j, e, st: (b, 0, 0))],
            out_specs=pl.BlockSpec((1, tb, d), lambda b, j, e, st: (b, j, 0)),
            scratch_shapes=[pltpu.VMEM((tb, d), F32)]),
        out_shape=jax.ShapeDtypeStruct((bsz, s, d), F32),
        compiler_params=_params(("parallel", "parallel", "arbitrary")),
        name="moe_scatter",
    )(starts, codet, affc, y, x1, gt2)


def _tile(s, pref):
    t = pref
    while s % t:
        t //= 2
    return t


def _layer(x, c, w_ada, b_ada, norm1_w, w_in, q_norm_w, k_norm_w, conv_w, a_log, dt_bias, dn_norm_w,
           w_attn_up, w_dn_up, w_o, norm2_w, w_router, w_gate, w_up, w_down):
    bsz, s, d = x.shape
    mod = _ada(c, w_ada, b_ada)
    sh1, sc1, gt1, sh2, sc2, gt2 = [m.reshape(bsz, 1, d) for m in jnp.split(mod, 6, axis=-1)]

    o0 = ATTN_WIDTH + 2 * KV_WIDTH
    o1 = o0 + 3 * DN_WIDTH
    o2 = o1 + DN_WIDTH
    o3 = o2 + 2 * DN_HEADS
    o4 = o3 + 2 * DN_HEADS
    pad = lambda w: jnp.zeros((d, LANES), w.dtype).at[:, :w.shape[1]].set(w)
    ws = [w_in[:, :o0], w_in[:, o0:o1], w_in[:, o1:o2], pad(w_in[:, o2:o3]), pad(w_in[:, o3:o4]), w_in[:, o4:]]
    ws = [w.astype(BF16) for w in ws]
    a_qkv, dqkv, dz, braw, araw, graw = _inproj(x, norm1_w.reshape(1, d), sh1, sc1, ws, _tile(s, 512))

    q, kt, v = _attn_prep(a_qkv, q_norm_w, k_norm_w, _tile(s, 512))
    attn = _attention(q, kt, v, _tile(s, 256), _tile(s, 512))

    ts = 512
    prep = _dn_prep(dqkv, braw, araw, conv_w, a_log, dt_bias, ts)
    o_f = _dn_scan(prep, False, 2 * CHUNK, ts // CHUNK)
    o_b = _dn_scan(prep, True, 2 * CHUNK, ts // CHUNK)

    x1, h2, affc, afft = _post(attn, o_f, o_b, dz, graw, x, gt1, sh2, sc2, dn_norm_w, norm2_w.reshape(1, d),
                               w_attn_up.astype(BF16), w_dn_up.astype(BF16), w_o.astype(BF16), w_router,
                               _tile(s, 512))

    cap = CAPACITY_FACTOR * s // N_EXPERTS
    tb = _tile(s, 256)
    win = tb + 16
    code = _topk(afft, cap)
    starts = ((code[:, :, ::tb] >> 1) // 16 * 16).reshape(-1)
    y = _moe_ffn(starts, code, h2, w_gate[None], w_up[None], w_down[None], cap, tb, win)
    return _moe_scatter(starts, jnp.transpose(code, (0, 2, 1)), affc, y, x1, gt2, tb, win)


def kernel(x, c, w_ada, b_ada, norm1_w, w_in, q_norm_w, k_norm_w, conv_w, a_log, dt_bias, dn_norm_w,
           w_attn_up, w_dn_up, w_o, norm2_w, w_router, w_gate, w_up, w_down):
    depth = w_ada.shape[0]
    for l in range(depth):
        x = _layer(x, c, w_ada[l], b_ada[l], norm1_w[l], w_in[l], q_norm_w[l], k_norm_w[l], conv_w[l],
                   a_log[l], dt_bias[l], dn_norm_w[l], w_attn_up[l], w_dn_up[l], w_o[l], norm2_w[l],
                   w_router[l], w_gate[l], w_up[l], w_down[l])
    return x
```

```python
import functools

import numpy as np
import jax
import jax.numpy as jnp
from jax import lax
from jax.experimental import pallas as pl
from jax.experimental.pallas import tpu as pltpu

F32 = jnp.float32
BF16 = jnp.bfloat16

GRID_W = 64
N_Q_HEADS = 8
N_KV_HEADS = 2
HEAD_DIM = 64
ATTN_WIDTH = N_Q_HEADS * HEAD_DIM
KV_WIDTH = N_KV_HEADS * HEAD_DIM
ROPE_THETA = 10000.0
DN_HEADS = 8
DN_HEAD_DIM = 64
DN_WIDTH = DN_HEADS * DN_HEAD_DIM
CONV_WIDTH = 5
CHUNK = 64
N_EXPERTS = 16
CAPACITY_FACTOR = 2
EPS = 1e-6
LOG2E = 1.4426950408889634
LANES = 128
VMEM_LIMIT = 56 * 1024 * 1024

NT_DIMS = (((1,), (1,)), ((), ()))


def _params(sem):
    return pltpu.CompilerParams(dimension_semantics=sem, vmem_limit_bytes=VMEM_LIMIT)


def _dot(a, b):
    return jnp.dot(a, b, preferred_element_type=F32)


def _split_dot(x, m, parts):
    acc = None
    r = x
    for i in range(parts):
        h = r.astype(BF16)
        d = _dot(h, m)
        acc = d if acc is None else acc + d
        if i + 1 < parts:
            r = r - h.astype(F32)
    return acc


def _split_dot_l(m, x, parts):
    acc = None
    r = x
    for i in range(parts):
        h = r.astype(BF16)
        d = _dot(m, h)
        acc = d if acc is None else acc + d
        if i + 1 < parts:
            r = r - h.astype(F32)
    return acc


def _silu(x):
    return x * jax.nn.sigmoid(x)


def _ada_kernel(c_ref, w_ref, b_ref, o_ref):
    c = c_ref[...]
    o_ref[...] = jnp.dot(_silu(c), w_ref[...], preferred_element_type=F32,
                         precision=lax.Precision.HIGHEST) + b_ref[...]


def _ada(c, w, b):
    bsz, d = c.shape
    n = w.shape[1]
    tn = 1536
    rows = 8
    c8 = jnp.zeros((rows, d), F32).at[:bsz].set(c)
    out = pl.pallas_call(
        _ada_kernel,
        grid=(n // tn,),
        in_specs=[pl.BlockSpec((rows, d), lambda j: (0, 0)),
                  pl.BlockSpec((d, tn), lambda j: (0, j)),
                  pl.BlockSpec((1, tn), lambda j: (0, j))],
        out_specs=pl.BlockSpec((rows, tn), lambda j: (0, j)),
        out_shape=jax.ShapeDtypeStruct((rows, n), F32),
        compiler_params=_params(("arbitrary",)),
        name="ada",
    )(c8, w, b.reshape(1, n))
    return out[:bsz]


def _inproj_kernel(x_ref, nw_ref, sh_ref, sc_ref, wa_ref, wd_ref, wz_ref, wb_ref, wal_ref, wg_ref,
                   oa_ref, od_ref, oz_ref, ob_ref, oal_ref, og_ref):
    x = x_ref[0]
    ms = jnp.mean(x * x, axis=-1, keepdims=True)
    h = x * lax.rsqrt(ms + EPS) * nw_ref[...]
    h = h * (1.0 + sc_ref[0]) + sh_ref[0]
    hb = h.astype(BF16)
    oa_ref[0] = _dot(hb, wa_ref[...]).astype(BF16)
    od_ref[0] = _dot(hb, wd_ref[...]).astype(BF16)
    oz_ref[0] = _dot(hb, wz_ref[...]).astype(BF16)
    ob_ref[0] = _dot(hb, wb_ref[...])
    oal_ref[0] = _dot(hb, wal_ref[...])
    og_ref[0] = _dot(hb, wg_ref[...]).astype(BF16)


def _inproj(x, nw, sh, sc, ws, tm):
    bsz, s, d = x.shape
    widths = [w.shape[1] for w in ws]
    dts = [BF16, BF16, BF16, F32, F32, BF16]
    tok = lambda w: pl.BlockSpec((1, tm, w), lambda b, i: (b, i, 0))
    full = lambda a: pl.BlockSpec(a.shape, lambda b, i: (0,) * a.ndim)
    vec = pl.BlockSpec((1, 1, d), lambda b, i: (b, 0, 0))
    return pl.pallas_call(
        _inproj_kernel,
        grid=(bsz, s // tm),
        in_specs=[tok(d), full(nw), vec, vec] + [full(w) for w in ws],
        out_specs=[tok(w) for w in widths],
        out_shape=[jax.ShapeDtypeStruct((bsz, s, w), dt) for w, dt in zip(widths, dts)],
        compiler_params=_params(("parallel", "parallel")),
        name="inproj",
    )(x, nw, sh, sc, *ws)


def _norm_rope(x, w, bd, cos, sa, sb):
    n = x.shape[1]
    ms = _split_dot(x * x, bd, 2)
    xn = x * lax.rsqrt(ms + EPS) * w
    return xn * cos + pltpu.roll(xn, n - 16, 1) * sa + pltpu.roll(xn, 16, 1) * sb


def _attn_prep_kernel(a_ref, cos_ref, sa_ref, sb_ref, qw_ref, kw_ref, bdq_ref, bdk_ref,
                      q_ref, kt_ref, v_ref):
    a = a_ref[0].astype(F32)
    q = a[:, :ATTN_WIDTH]
    k = a[:, ATTN_WIDTH:ATTN_WIDTH + KV_WIDTH]
    v = a[:, ATTN_WIDTH + KV_WIDTH:]
    cos, sa, sb = cos_ref[...], sa_ref[...], sb_ref[...]
    rep = ATTN_WIDTH // LANES
    tile = lambda t: jnp.concatenate([t] * rep, axis=1)
    qr = _norm_rope(q, qw_ref[...], bdq_ref[...], tile(cos), tile(sa), tile(sb))
    q_ref[0] = (qr * (HEAD_DIM ** -0.5 * LOG2E)).astype(BF16)
    kr = _norm_rope(k, kw_ref[...], bdk_ref[...], cos, sa, sb)
    kt = kr.T
    kt_ref[0, 0] = kt[:HEAD_DIM].astype(BF16)
    kt_ref[0, 1] = kt[HEAD_DIM:].astype(BF16)
    lane = lax.broadcasted_iota(jnp.int32, v.shape, 1)
    ones_col = jnp.where(lane == HEAD_DIM, 1.0, 0.0)
    v_ref[0, 0] = jnp.where(lane < HEAD_DIM, v, ones_col).astype(BF16)
    v_ref[0, 1] = jnp.where(lane < HEAD_DIM, pltpu.roll(v, HEAD_DIM, 1), ones_col).astype(BF16)


def _rope_tables(s):
    pos = jnp.arange(s)
    lane = jnp.arange(LANES)
    d = lane % HEAD_DIM
    p = d % 32
    f = (p % 16).astype(F32)
    freqs = ROPE_THETA ** (-(f * 2.0 / 32.0))
    axis_pos = jnp.where((d // 32)[None, :] == 0, (pos // GRID_W)[:, None], (pos % GRID_W)[:, None])
    ang = axis_pos.astype(F32) * freqs[None, :]
    cos, sin = jnp.cos(ang), jnp.sin(ang)
    first = (p < 16)[None, :]
    return cos, jnp.where(first, -sin, 0.0), jnp.where(first, 0.0, sin)


def _block_diag(n, blk, val):
    i = np.arange(n)
    return jnp.asarray(np.where((i[:, None] // blk) == (i[None, :] // blk), val, 0.0), BF16)


def _attn_prep(a, qw, kw, tp):
    bsz, s, wa = a.shape
    cos, sa, sb = _rope_tables(s)
    bdq = _block_diag(ATTN_WIDTH, HEAD_DIM, 1.0 / HEAD_DIM)
    bdk = _block_diag(KV_WIDTH, HEAD_DIM, 1.0 / HEAD_DIM)
    qw_t = jnp.tile(qw, N_Q_HEADS).reshape(1, ATTN_WIDTH)
    kw_t = jnp.tile(kw, N_KV_HEADS).reshape(1, KV_WIDTH)
    tab = pl.BlockSpec((tp, LANES), lambda b, i: (i, 0))
    full = lambda t: pl.BlockSpec(t.shape, lambda b, i: (0,) * t.ndim)
    return pl.pallas_call(
        _attn_prep_kernel,
        grid=(bsz, s // tp),
        in_specs=[pl.BlockSpec((1, tp, wa), lambda b, i: (b, i, 0)), tab, tab, tab,
                  full(qw_t), full(kw_t), full(bdq), full(bdk)],
        out_specs=[pl.BlockSpec((1, tp, ATTN_WIDTH), lambda b, i: (b, i, 0)),
                   pl.BlockSpec((1, N_KV_HEADS, HEAD_DIM, tp), lambda b, i: (b, 0, 0, i)),
                   pl.BlockSpec((1, N_KV_HEADS, tp, LANES), lambda b, i: (b, 0, i, 0))],
        out_shape=[jax.ShapeDtypeStruct((bsz, s, ATTN_WIDTH), BF16),
                   jax.ShapeDtypeStruct((bsz, N_KV_HEADS, HEAD_DIM, s), BF16),
                   jax.ShapeDtypeStruct((bsz, N_KV_HEADS, s, LANES), BF16)],
        compiler_params=_params(("parallel", "parallel")),
        name="attn_prep",
    )(a, cos, sa, sb, qw_t, kw_t, bdq, bdk)


def _attn_kernel(q_ref, kt_ref, v_ref, o_ref, qs, m_s, acc, *, tq, group):
    ki = pl.program_id(3)

    @pl.when(ki == 0)
    def _():
        q = q_ref[0]
        for h in range(group):
            qs[h * tq:(h + 1) * tq, :] = q[:, h * HEAD_DIM:(h + 1) * HEAD_DIM]
        m_s[...] = jnp.full(m_s.shape, -jnp.inf, F32)
        acc[...] = jnp.zeros(acc.shape, F32)

    s = _dot(qs[...], kt_ref[0, 0])
    m_old = m_s[...]
    m_new = jnp.maximum(m_old, jnp.max(s, axis=1, keepdims=True))
    alpha = jnp.exp2(m_old - m_new)
    tiles = s.shape[1] // LANES
    p = jnp.concatenate([jnp.exp2(s[:, t * LANES:(t + 1) * LANES] - m_new).astype(BF16)
                         for t in range(tiles)], axis=1)
    acc[...] = alpha * acc[...] + _dot(p, v_ref[0, 0])
    m_s[...] = m_new

    @pl.when(ki == pl.num_programs(3) - 1)
    def _():
        a = acc[...]
        o = a[:, :HEAD_DIM] * (1.0 / a[:, HEAD_DIM:HEAD_DIM + 1])
        o_ref[0] = jnp.concatenate([o[h * tq:(h + 1) * tq] for h in range(group)], axis=1).astype(BF16)


def _attention(q, kt, v, tq, tk):
    bsz, s, _ = q.shape
    group = N_Q_HEADS // N_KV_HEADS
    gw = group * HEAD_DIM
    return pl.pallas_call(
        functools.partial(_attn_kernel, tq=tq, group=group),
        grid=(bsz, N_KV_HEADS, s // tq, s // tk),
        in_specs=[pl.BlockSpec((1, tq, gw), lambda b, g, i, j: (b, i, g)),
                  pl.BlockSpec((1, 1, HEAD_DIM, tk), lambda b, g, i, j: (b, g, 0, j)),
                  pl.BlockSpec((1, 1, tk, LANES), lambda b, g, i, j: (b, g, j, 0))],
        out_specs=pl.BlockSpec((1, tq, gw), lambda b, g, i, j: (b, i, g)),
        out_shape=jax.ShapeDtypeStruct((bsz, s, ATTN_WIDTH), BF16),
        scratch_shapes=[pltpu.VMEM((group * tq, HEAD_DIM), BF16),
                        pltpu.VMEM((group * tq, LANES), F32),
                        pltpu.VMEM((group * tq, LANES), F32)],
        compiler_params=_params(("parallel", "parallel", "parallel", "arbitrary")),
        name="attention",
    )(q, kt, v)


def _dn_prep_kernel(cur_ref, prev_ref, next_ref, cw_ref, braw_ref, araw_ref, alog_ref, dtb_ref,
                    bd_ref, trif_ref, trib_ref, tot_ref, sel_ref, exf_ref, exb_ref,
                    kn_ref, qn_ref, vb_ref, win_ref, qg_ref, kdt_ref, gc_ref, gct_ref, beta_ref, gl_ref,
                    ext, *, ts):
    i = pl.program_id(1)
    halo = 16
    w3 = 3 * DN_WIDTH
    ext[halo:halo + ts, :] = cur_ref[0].astype(F32)
    ext[0:halo, :] = jnp.where(i > 0, prev_ref[0].astype(F32), 0.0)
    ext[halo + ts:, :] = jnp.where(i < pl.num_programs(1) - 1, next_ref[0].astype(F32), 0.0)
    cw = cw_ref[...]
    conv = jnp.zeros((ts, w3), F32)
    for j in range(CONV_WIDTH):
        off = halo - CONV_WIDTH // 2 + j
        conv = conv + ext[off:off + ts, :] * cw[j:j + 1, :]
    act = _silu(conv)
    cq, ck, cv = act[:, :DN_WIDTH], act[:, DN_WIDTH:2 * DN_WIDTH], act[:, 2 * DN_WIDTH:]
    bd = bd_ref[...]
    qn = cq * lax.rsqrt(_split_dot(cq * cq, bd, 2) + EPS) * (DN_HEAD_DIM ** -0.5)
    kn = ck * lax.rsqrt(_split_dot(ck * ck, bd, 2) + EPS)
    kn_ref[0] = kn.astype(BF16)
    qn_ref[0] = qn.astype(BF16)

    beta = jax.nn.sigmoid(braw_ref[0])
    g = -jnp.exp(alog_ref[...]) * jax.nn.softplus(araw_ref[0] + dtb_ref[...])
    gtot = _split_dot_l(tot_ref[...], g, 3)
    gl = jnp.exp(_split_dot_l(sel_ref[...], g, 3))
    for d, (tri_ref, ex_ref) in enumerate(((trif_ref, exf_ref), (trib_ref, exb_ref))):
        gc = _split_dot_l(tri_ref[...], g, 3)
        ex = ex_ref[...]
        beta_x = _split_dot(beta, ex, 2)
        eg_x = _split_dot(jnp.exp(gc), ex, 2)
        ek_x = _split_dot(jnp.exp(gtot - gc), ex, 2)
        vb_ref[d, 0] = (cv * beta_x).astype(BF16)
        win_ref[d, 0] = (kn * beta_x * eg_x).astype(BF16)
        qg_ref[d, 0] = (qn * eg_x).astype(BF16)
        kdt_ref[d, 0] = (kn * ek_x).T.astype(BF16)
        shift = (LANES - d * DN_HEADS) % LANES
        gc_d = gc if d == 0 else pltpu.roll(gc, shift, 1)
        gc_ref[d, 0] = gc_d
        gct_ref[d, 0] = gc_d.T[:DN_HEADS]
        beta_ref[d, 0] = beta if d == 0 else pltpu.roll(beta, shift, 1)
        gl_ref[d, 0] = _split_dot(gl, ex, 2)


def _dn_prep(dqkv, braw, araw, conv_w, a_log, dt_bias, ts):
    bsz, s, w3 = dqkv.shape
    nt = s // ts
    cpt = ts // CHUNK
    assert cpt == 8
    halo = 16
    hb = ts // halo
    idx = np.arange(ts)
    same = (idx[:, None] // CHUNK) == (idx[None, :] // CHUNK)
    trif = jnp.asarray(np.where(same & (idx[:, None] >= idx[None, :]), 1.0, 0.0), BF16)
    trib = jnp.asarray(np.where(same & (idx[:, None] <= idx[None, :]), 1.0, 0.0), BF16)
    tot = jnp.asarray(np.where(same, 1.0, 0.0), BF16)
    sel = jnp.asarray(np.where(np.arange(cpt)[:, None] == (idx[None, :] // CHUNK), 1.0, 0.0), BF16)
    bd = _block_diag(DN_WIDTH, DN_HEAD_DIM, 1.0)
    lane = np.arange(DN_WIDTH) // DN_HEAD_DIM
    row = np.arange(LANES)
    exf = jnp.asarray(np.where(row[:, None] == lane[None, :], 1.0, 0.0), BF16)
    exb = jnp.asarray(np.where(row[:, None] == lane[None, :] + DN_HEADS, 1.0, 0.0), BF16)
    nh2 = 2 * DN_HEADS
    alog = jnp.zeros((1, LANES), F32).at[0, :nh2].set(a_log.reshape(nh2))
    dtb = jnp.zeros((1, LANES), F32).at[0, :nh2].set(dt_bias.reshape(nh2))
    full = lambda t: pl.BlockSpec(t.shape, lambda b, i: (0,) * t.ndim)
    tok = lambda w: pl.BlockSpec((1, ts, w), lambda b, i: (b, i, 0))
    dtok = lambda w: pl.BlockSpec((2, 1, ts, w), lambda b, i: (0, b, i, 0))
    sds = jax.ShapeDtypeStruct
    return pl.pallas_call(
        functools.partial(_dn_prep_kernel, ts=ts),
        grid=(bsz, nt),
        in_specs=[tok(w3),
                  pl.BlockSpec((1, halo, w3), lambda b, i: (b, jnp.maximum(i * hb - 1, 0), 0)),
                  pl.BlockSpec((1, halo, w3), lambda b, i: (b, jnp.minimum((i + 1) * hb, s // halo - 1), 0)),
                  full(conv_w), tok(LANES), tok(LANES), full(alog), full(dtb),
                  full(bd), full(trif), full(trib), full(tot), full(sel), full(exf), full(exb)],
        out_specs=[tok(DN_WIDTH), tok(DN_WIDTH), dtok(DN_WIDTH), dtok(DN_WIDTH), dtok(DN_WIDTH),
                   pl.BlockSpec((2, 1, DN_WIDTH, ts), lambda b, i: (0, b, 0, i)),
                   dtok(LANES),
                   pl.BlockSpec((2, 1, DN_HEADS, ts), lambda b, i: (0, b, 0, i)),
                   dtok(LANES),
                   pl.BlockSpec((2, 1, cpt, DN_WIDTH), lambda b, i: (0, b, i, 0))],
        out_shape=[sds((bsz, s, DN_WIDTH), BF16), sds((bsz, s, DN_WIDTH), BF16),
                   sds((2, bsz, s, DN_WIDTH), BF16), sds((2, bsz, s, DN_WIDTH), BF16),
                   sds((2, bsz, s, DN_WIDTH), BF16), sds((2, bsz, DN_WIDTH, s), BF16),
                   sds((2, bsz, s, LANES), F32), sds((2, bsz, DN_HEADS, s), F32),
                   sds((2, bsz, s, LANES), F32), sds((2, bsz, s // CHUNK, DN_WIDTH), F32)],
        scratch_shapes=[pltpu.VMEM((ts + 2 * halo, w3), F32)],
        compiler_params=_params(("parallel", "parallel")),
        name="dn_prep",
    )(dqkv, dqkv, dqkv, conv_w, braw, araw, alog, dtb, bd, trif, trib, tot, sel, exf, exb)


def _dn_scan_kernel(kn_ref, qn_ref, vb_ref, win_ref, qg_ref, kdt_ref, gc_ref, gct_ref, beta_ref, gl_ref,
                    o_ref, state, *, rev, cpb, gl_rows):
    j = pl.program_id(1)
    nb = pl.num_programs(1)

    @pl.when(j == 0)
    def _():
        state[...] = jnp.zeros(state.shape, F32)

    jj = (nb - 1 - j) if rev else j
    ri = lax.broadcasted_iota(jnp.int32, (CHUNK, CHUNK), 0)
    ci = lax.broadcasted_iota(jnp.int32, (CHUNK, CHUNK), 1)
    incl = (ri <= ci) if rev else (ri >= ci)
    strict = (ri < ci) if rev else (ri > ci)
    eye = jnp.where(ri == ci, 1.0, 0.0)
    merge = [((ri // (2 * sz)) == (ci // (2 * sz))) & ((ri // sz) != (ci // sz))
             for sz in (1, 2, 4, 8, 16, 32)]

    order = list(range(cpb - 1, -1, -1) if rev else range(cpb))
    heads = range(DN_HEADS)
    items = [(c, h) for c in order for h in heads]
    rows = {c: slice(c * CHUNK, (c + 1) * CHUNK) for c in order}
    ls = {h: slice(h * DN_HEAD_DIM, (h + 1) * DN_HEAD_DIM) for h in heads}
    k_c = {c: kn_ref[0, rows[c], :] for c in order}
    q_c = {c: qn_ref[0, rows[c], :] for c in order}
    gc_c = {c: gc_ref[0, 0, rows[c], :] for c in order}
    gct_c = {c: gct_ref[0, 0, :, rows[c]] for c in order}
    beta_c = {c: beta_ref[0, 0, rows[c], :] for c in order}
    a_kk = {(c, h): lax.dot_general(k_c[c][:, ls[h]], k_c[c][:, ls[h]], NT_DIMS, preferred_element_type=F32)
            for c, h in items}
    a_qk = {(c, h): lax.dot_general(q_c[c][:, ls[h]], k_c[c][:, ls[h]], NT_DIMS, preferred_element_type=F32)
            for c, h in items}
    dec, lm, t = {}, {}, {}
    for c, h in items:
        diff = gc_c[c][:, h:h + 1] - gct_c[c][h:h + 1, :]
        dec[c, h] = jnp.exp(jnp.where(incl, diff, -jnp.inf))
        lm[c, h] = jnp.where(strict, a_kk[c, h] * beta_c[c][:, h:h + 1] * dec[c, h], 0.0)
        t[c, h] = eye - jnp.where(merge[0], lm[c, h], 0.0)
    for mk in merge[1:]:
        tb = {i: t[i].astype(BF16) for i in items}
        x = {i: _dot(jnp.where(mk, lm[i], 0.0).astype(BF16), tb[i]).astype(BF16) for i in items}
        t = {i: t[i] - _dot(tb[i], x[i]) for i in items}
    tb = {i: t[i].astype(BF16) for i in items}
    u = {(c, h): _dot(tb[c, h], vb_ref[0, 0, rows[c], ls[h]]) for c, h in items}
    w = {(c, h): _dot(tb[c, h], win_ref[0, 0, rows[c], ls[h]]).astype(BF16) for c, h in items}
    a_in = {i: (a_qk[i] * dec[i]).astype(BF16) for i in items}

    for c in order:
        gl_c = gl_ref[0, 0, pl.ds((jj * cpb) % gl_rows + c, 1), :]
        s_old = {h: state[h] for h in heads}
        sb = {h: s_old[h].astype(BF16) for h in heads}
        ws = {h: _dot(w[c, h], sb[h]) for h in heads}
        qs = {h: _dot(qg_ref[0, 0, rows[c], ls[h]], sb[h]) for h in heads}
        vnb = {h: (u[c, h] - ws[h]).astype(BF16) for h in heads}
        o_in = {h: _dot(a_in[c, h], vnb[h]) for h in heads}
        ds = {h: _dot(kdt_ref[0, 0, ls[h], rows[c]], vnb[h]) for h in heads}
        for h in heads:
            state[h] = s_old[h] * gl_c[:, ls[h]] + ds[h]
        o_ref[0, rows[c], :] = jnp.concatenate([qs[h] + o_in[h] for h in heads], axis=1)


def _dn_scan(prep, rev, tsb, gl_rows):
    kn, qn, vb, win, qg, kdt, gc, gct, beta, gl = prep
    bsz, s, _ = kn.shape
    nb = s // tsb
    cpb = tsb // CHUNK
    d = 1 if rev else 0
    blk = (lambda j: nb - 1 - j) if rev else (lambda j: j)
    tok = pl.BlockSpec((1, tsb, DN_WIDTH), lambda b, j: (b, blk(j), 0))
    dtok = lambda w: pl.BlockSpec((1, 1, tsb, w), lambda b, j: (d, b, blk(j), 0))
    return pl.pallas_call(
        functools.partial(_dn_scan_kernel, rev=rev, cpb=cpb, gl_rows=gl_rows),
        grid=(bsz, nb),
        in_specs=[tok, tok, dtok(DN_WIDTH), dtok(DN_WIDTH), dtok(DN_WIDTH),
                  pl.BlockSpec((1, 1, DN_WIDTH, tsb), lambda b, j: (d, b, 0, blk(j))),
                  dtok(LANES),
                  pl.BlockSpec((1, 1, DN_HEADS, tsb), lambda b, j: (d, b, 0, blk(j))),
                  dtok(LANES),
                  pl.BlockSpec((1, 1, gl_rows, DN_WIDTH), lambda b, j: (d, b, (blk(j) * cpb) // gl_rows, 0))],
        out_specs=tok,
        out_shape=jax.ShapeDtypeStruct((bsz, s, DN_WIDTH), F32),
        scratch_shapes=[pltpu.VMEM((DN_HEADS, DN_HEAD_DIM, DN_HEAD_DIM), F32)],
        compiler_params=_params(("parallel", "arbitrary")),
        name="dn_scan_bwd" if rev else "dn_scan_fwd",
    )(kn, qn, vb, win, qg, kdt, gc, gct, beta, gl)


def _post_kernel(attn_ref, of_ref, ob_ref, dz_ref, g_ref, x_ref, gt1_ref, sh2_ref, sc2_ref,
                 dnw_ref, n2w_ref, bd_ref, wau_ref, wdu_ref, wo_ref, wr_ref,
                 x1_ref, h2_ref, aff_ref, afft_ref):
    o = of_ref[0] + ob_ref[0]
    ms = _split_dot(o * o, bd_ref[...], 2)
    dn = o * lax.rsqrt(ms + EPS) * dnw_ref[...] * _silu(dz_ref[0].astype(F32))
    gates = jax.nn.sigmoid(g_ref[0].astype(F32))
    d = x_ref.shape[2]
    merged = (gates[:, :d] * _dot(attn_ref[0], wau_ref[...])
              + gates[:, d:] * _dot(dn.astype(BF16), wdu_ref[...]))
    x1 = x_ref[0] + gt1_ref[0] * _dot(merged.astype(BF16), wo_ref[...])
    x1_ref[0] = x1
    ms2 = jnp.mean(x1 * x1, axis=-1, keepdims=True)
    h2 = x1 * lax.rsqrt(ms2 + EPS) * n2w_ref[...]
    h2 = h2 * (1.0 + sc2_ref[0]) + sh2_ref[0]
    h2_ref[0] = h2.astype(BF16)
    logits = jnp.dot(h2, wr_ref[...], preferred_element_type=F32, precision=lax.Precision.HIGHEST)
    lane = lax.broadcasted_iota(jnp.int32, logits.shape, 1)
    logits = jnp.where(lane < N_EXPERTS, logits, -jnp.inf)
    e = jnp.exp(logits - jnp.max(logits, axis=1, keepdims=True))
    aff = e / jnp.sum(e, axis=1, keepdims=True)
    aff_ref[0] = aff[:, :N_EXPERTS]
    afft_ref[0] = aff.T[:N_EXPERTS]


def _post(attn, o_f, o_b, dz, graw, x, gt1, sh2, sc2, dnw, n2w, wau, wdu, wo, wr, tm):
    bsz, s, d = x.shape
    bd = _block_diag(DN_WIDTH, DN_HEAD_DIM, 1.0 / DN_HEAD_DIM)
    dnw_t = jnp.tile(dnw, DN_HEADS).reshape(1, DN_WIDTH)
    wr_p = jnp.zeros((d, LANES), F32).at[:, :N_EXPERTS].set(wr)
    tok = lambda w: pl.BlockSpec((1, tm, w), lambda b, i: (b, i, 0))
    full = lambda t: pl.BlockSpec(t.shape, lambda b, i: (0,) * t.ndim)
    vec = pl.BlockSpec((1, 1, d), lambda b, i: (b, 0, 0))
    sds = jax.ShapeDtypeStruct
    return pl.pallas_call(
        _post_kernel,
        grid=(bsz, s // tm),
        in_specs=[tok(ATTN_WIDTH), tok(DN_WIDTH), tok(DN_WIDTH), tok(DN_WIDTH), tok(2 * d), tok(d),
                  vec, vec, vec, full(dnw_t), full(n2w), full(bd), full(wau), full(wdu), full(wo), full(wr_p)],
        out_specs=[tok(d), tok(d), tok(N_EXPERTS),
                   pl.BlockSpec((1, N_EXPERTS, tm), lambda b, i: (b, 0, i))],
        out_shape=[sds((bsz, s, d), F32), sds((bsz, s, d), BF16), sds((bsz, s, N_EXPERTS), F32),
                   sds((bsz, N_EXPERTS, s), F32)],
        compiler_params=_params(("parallel", "parallel")),
        name="post_mixer",
    )(attn, o_f, o_b, dz, graw, x, gt1, sh2, sc2, dnw_t, n2w, bd, wau, wdu, wo, wr_p)


def _topk_kernel(aff_ref, tri_ref, code_ref, *, cap, s):
    a = aff_ref[0]
    ne = a.shape[0]
    count = lambda m: jnp.sum(jnp.where(m, 1.0, 0.0), axis=1, keepdims=True)

    def vbody(i, lo):
        cand = lo | (jnp.int32(1) << (30 - i))
        return jnp.where(count(a >= pltpu.bitcast(cand, F32)) >= cap, cand, lo)

    lo_bits = lax.fori_loop(0, 31, vbody, jnp.zeros((ne, 1), jnp.int32))
    lo = pltpu.bitcast(lo_bits, F32)
    hi = pltpu.bitcast(lo_bits + 1, F32)

    def rbody(i, lh):
        lo, hi = lh
        mid = 0.5 * (lo + hi)
        ok = count(a >= mid) >= cap
        return jnp.where(ok, mid, lo), jnp.where(ok, hi, mid)

    lo, hi = lax.fori_loop(0, 32, rbody, (lo, hi))
    gt = a >= hi
    eq = (a >= lo) & jnp.logical_not(gt)
    need = cap - count(gt)
    idx = lax.broadcasted_iota(jnp.int32, a.shape, 1)
    nbits = int(np.log2(s))

    def ibody(i, x):
        cand = x | (jnp.int32(1) << (nbits - 1 - i))
        return jnp.where(count(eq & (idx < cand)) < need, cand, x)

    last = lax.fori_loop(0, nbits, ibody, jnp.zeros((ne, 1), jnp.int32))
    sel = gt | (eq & (idx <= last))
    self32 = jnp.where(sel, 1.0, 0.0)
    tri = tri_ref[...]
    carry = jnp.zeros((ne, 1), F32)
    for t in range(s // LANES):
        seg = self32[:, t * LANES:(t + 1) * LANES]
        inc = _dot(seg.astype(BF16), tri)
        pos = (inc - seg + carry).astype(jnp.int32)
        code_ref[0, :, t * LANES:(t + 1) * LANES] = pos * 2 + seg.astype(jnp.int32)
        carry = carry + inc[:, LANES - 1:LANES]


def _topk(afft, cap):
    bsz, ne, s = afft.shape
    i = np.arange(LANES)
    tri = jnp.asarray(np.where(i[:, None] <= i[None, :], 1.0, 0.0), BF16)
    return pl.pallas_call(
        functools.partial(_topk_kernel, cap=cap, s=s),
        grid=(bsz,),
        in_specs=[pl.BlockSpec((1, ne, s), lambda b: (b, 0, 0)),
                  pl.BlockSpec((LANES, LANES), lambda b: (0, 0))],
        out_specs=pl.BlockSpec((1, ne, s), lambda b: (b, 0, 0)),
        out_shape=jax.ShapeDtypeStruct((bsz, ne, s), jnp.int32),
        compiler_params=_params(("parallel",)),
        name="topk",
    )(afft, tri)


def _moe_ffn_kernel(starts_ref, code_ref, h2_ref, wg_ref, wu_ref, wd_ref, y_ref, xe, *, tb, win, cap, nj):
    b, e, j = pl.program_id(0), pl.program_id(1), pl.program_id(2)

    @pl.when(j == 0)
    def _():
        xe[...] = jnp.zeros(xe.shape, F32)

    a = pl.multiple_of(starts_ref[(b * pl.num_programs(1) + e) * nj + j], 16)
    code = code_ref[0, 0, 0]
    r = lax.broadcasted_iota(jnp.int32, (win, tb), 0) + a
    onehot = jnp.where((code == 2 * r + 1), 1.0, 0.0).astype(BF16)
    xe[pl.ds(a, win), :] = xe[pl.ds(a, win), :] + _dot(onehot, h2_ref[0])

    @pl.when(j == nj - 1)
    def _():
        xb = xe[0:cap, :].astype(BF16)
        hg = _dot(xb, wg_ref[0, 0].astype(BF16))
        hu = _dot(xb, wu_ref[0, 0].astype(BF16))
        act = (_silu(hg) * hu).astype(BF16)
        y_ref[0, 0, 0:cap, :] = _dot(act, wd_ref[0, 0].astype(BF16)).astype(BF16)
        y_ref[0, 0, cap:, :] = jnp.zeros((y_ref.shape[2] - cap, y_ref.shape[3]), BF16)


def _moe_ffn(starts, code, h2, w_gate, w_up, w_down, cap, tb, win):
    bsz, s, d = h2.shape
    ne = code.shape[1]
    nj = s // tb
    f = w_gate.shape[-1]
    capp = cap + win
    code5 = code.reshape(bsz, ne, nj, 1, tb)
    wspec = lambda r, c: pl.BlockSpec((1, 1, r, c), lambda b, e, j, st: (0, e, 0, 0))
    return pl.pallas_call(
        functools.partial(_moe_ffn_kernel, tb=tb, win=win, cap=cap, nj=nj),
        grid_spec=pltpu.PrefetchScalarGridSpec(
            num_scalar_prefetch=1,
            grid=(bsz, ne, nj),
            in_specs=[pl.BlockSpec((1, 1, 1, 1, tb), lambda b, e, j, st: (b, e, j, 0, 0)),
                      pl.BlockSpec((1, tb, d), lambda b, e, j, st: (b, j, 0)),
                      wspec(d, f), wspec(d, f), wspec(f, d)],
            out_specs=pl.BlockSpec((1, 1, capp, d), lambda b, e, j, st: (b, e, 0, 0)),
            scratch_shapes=[pltpu.VMEM((capp, d), F32)]),
        out_shape=jax.ShapeDtypeStruct((bsz, ne, capp, d), BF16),
        compiler_params=_params(("parallel", "parallel", "arbitrary")),
        name="moe_ffn",
    )(starts, code5, h2, w_gate, w_up, w_down)


def _moe_scatter_kernel(starts_ref, codet_ref, affc_ref, y_ref, x1_ref, gt2_ref, o_ref, acc, *, tb, win, nj):
    b, j, e = pl.program_id(0), pl.program_id(1), pl.program_id(2)
    ne = pl.num_programs(2)

    @pl.when(e == 0)
    def _():
        acc[...] = jnp.zeros(acc.shape, F32)

    a = starts_ref[(b * ne + e) * nj + j]
    lane = lax.broadcasted_iota(jnp.int32, codet_ref.shape[1:], 1)
    pick = lane == e
    code = jnp.sum(jnp.where(pick, codet_ref[0].astype(F32), 0.0), axis=1, keepdims=True).astype(jnp.int32)
    gate = jnp.sum(jnp.where(pick, affc_ref[0], 0.0), axis=1, keepdims=True)
    r = lax.broadcasted_iota(jnp.int32, (tb, win), 1) + a
    onehot = jnp.where(code == 2 * r + 1, 1.0, 0.0).astype(BF16)
    acc[...] = acc[...] + gate * _dot(onehot, y_ref[...])

    @pl.when(e == ne - 1)
    def _():
        o_ref[0] = x1_ref[0] + gt2_ref[0] * acc[...]


def _moe_scatter(starts, codet, affc, y, x1, gt2, tb, win):
    bsz, s, d = x1.shape
    ne = codet.shape[2]
    nj = s // tb
    return pl.pallas_call(
        functools.partial(_moe_scatter_kernel, tb=tb, win=win, nj=nj),
        grid_spec=pltpu.PrefetchScalarGridSpec(
            num_scalar_prefetch=1,
            grid=(bsz, nj, ne),
            in_specs=[pl.BlockSpec((1, tb, ne), lambda b, j, e, st: (b, j, 0)),
                      pl.BlockSpec((1, tb, ne), lambda b, j, e, st: (b, j, 0)),
                      pl.BlockSpec((pl.squeezed, pl.squeezed, pl.Element(win), pl.Element(d)),
                                   lambda b, j, e, st: (b, e, pl.multiple_of(st[(b * ne + e) * nj + j], 16), 0)),
                      pl.BlockSpec((1, tb, d), lambda b, j, e, st: (b, j, 0)),
                      pl.BlockSpec((1, 1, d), lambda b, j, e, st: (b, 0, 0))],
            out_specs=pl.BlockSpec((1, tb, d), lambda b, j, e, st: (b, j, 0)),
            scratch_shapes=[pltpu.VMEM((tb, d), F32)]),
        out_shape=jax.ShapeDtypeStruct((bsz, s, d), F32),
        compiler_params=_params(("parallel", "parallel", "arbitrary")),
        name="moe_scatter",
    )(starts, codet, affc, y, x1, gt2)


def _tile(s, pref):
    t = pref
    while s % t:
        t //= 2
    return t


def _layer(x, c, w_ada, b_ada, norm1_w, w_in, q_norm_w, k_norm_w, conv_w, a_log, dt_bias, dn_norm_w,
           w_attn_up, w_dn_up, w_o, norm2_w, w_router, w_gate, w_up, w_down):
    bsz, s, d = x.shape
    mod = _ada(c, w_ada, b_ada)
    sh1, sc1, gt1, sh2, sc2, gt2 = [m.reshape(bsz, 1, d) for m in jnp.split(mod, 6, axis=-1)]

    o0 = ATTN_WIDTH + 2 * KV_WIDTH
    o1 = o0 + 3 * DN_WIDTH
    o2 = o1 + DN_WIDTH
    o3 = o2 + 2 * DN_HEADS
    o4 = o3 + 2 * DN_HEADS
    pad = lambda w: jnp.zeros((d, LANES), w.dtype).at[:, :w.shape[1]].set(w)
    ws = [w_in[:, :o0], w_in[:, o0:o1], w_in[:, o1:o2], pad(w_in[:, o2:o3]), pad(w_in[:, o3:o4]), w_in[:, o4:]]
    ws = [w.astype(BF16) for w in ws]
    a_qkv, dqkv, dz, braw, araw, graw = _inproj(x, norm1_w.reshape(1, d), sh1, sc1, ws, _tile(s, 512))

    q, kt, v = _attn_prep(a_qkv, q_norm_w, k_norm_w, _tile(s, 512))
    attn = _attention(q, kt, v, _tile(s, 256), _tile(s, 1024))

    ts = 512
    prep = _dn_prep(dqkv, braw, araw, conv_w, a_log, dt_bias, ts)
    o_f = _dn_scan(prep, False, 2 * CHUNK, ts // CHUNK)
    o_b = _dn_scan(prep, True, 2 * CHUNK, ts // CHUNK)

    x1, h2, affc, afft = _post(attn, o_f, o_b, dz, graw, x, gt1, sh2, sc2, dn_norm_w, norm2_w.reshape(1, d),
                               w_attn_up.astype(BF16), w_dn_up.astype(BF16), w_o.astype(BF16), w_router,
                               _tile(s, 512))

    cap = CAPACITY_FACTOR * s // N_EXPERTS
    tb = _tile(s, 256)
    win = tb + 16
    code = _topk(afft, cap)
    starts = ((code[:, :, ::tb] >> 1) // 16 * 16).reshape(-1)
    y = _moe_ffn(starts, code, h2, w_gate[None], w_up[None], w_down[None], cap, tb, win)
    return _moe_scatter(starts, jnp.transpose(code, (0, 2, 1)), affc, y, x1, gt2, tb, win)


def kernel(x, c, w_ada, b_ada, norm1_w, w_in, q_norm_w, k_norm_w, conv_w, a_log, dt_bias, dn_norm_w,
           w_attn_up, w_dn_up, w_o, norm2_w, w_router, w_gate, w_up, w_down):
    depth = w_ada.shape[0]
    for l in range(depth):
        x = _layer(x, c, w_ada[l], b_ada[l], norm1_w[l], w_in[l], q_norm_w[l], k_norm_w[l], conv_w[l],
                   a_log[l], dt_bias[l], dn_norm_w[l], w_attn_up[l], w_dn_up[l], w_o[l], norm2_w[l],
                   w_router[l], w_gate[l], w_up[l], w_down[l])
    return x
```

```python
import functools

import numpy as np
import jax
import jax.numpy as jnp
from jax import lax
from jax.experimental import pallas as pl
from jax.experimental.pallas import tpu as pltpu

F32 = jnp.float32
BF16 = jnp.bfloat16

GRID_W = 64
N_Q_HEADS = 8
N_KV_HEADS = 2
HEAD_DIM = 64
ATTN_WIDTH = N_Q_HEADS * HEAD_DIM
KV_WIDTH = N_KV_HEADS * HEAD_DIM
ROPE_THETA = 10000.0
DN_HEADS = 8
DN_HEAD_DIM = 64
DN_WIDTH = DN_HEADS * DN_HEAD_DIM
CONV_WIDTH = 5
CHUNK = 64
N_EXPERTS = 16
CAPACITY_FACTOR = 2
EPS = 1e-6
LOG2E = 1.4426950408889634
LANES = 128
VMEM_LIMIT = 56 * 1024 * 1024

NT_DIMS = (((1,), (1,)), ((), ()))


def _params(sem):
    return pltpu.CompilerParams(dimension_semantics=sem, vmem_limit_bytes=VMEM_LIMIT)


def _dot(a, b):
    return jnp.dot(a, b, preferred_element_type=F32)


def _split_dot(x, m, parts):
    acc = None
    r = x
    for i in range(parts):
        h = r.astype(BF16)
        d = _dot(h, m)
        acc = d if acc is None else acc + d
        if i + 1 < parts:
            r = r - h.astype(F32)
    return acc


def _split_dot_l(m, x, parts):
    acc = None
    r = x
    for i in range(parts):
        h = r.astype(BF16)
        d = _dot(m, h)
        acc = d if acc is None else acc + d
        if i + 1 < parts:
            r = r - h.astype(F32)
    return acc


def _silu(x):
    return x * jax.nn.sigmoid(x)


def _ada_kernel(c_ref, w_ref, b_ref, o_ref):
    c = c_ref[...]
    o_ref[...] = jnp.dot(_silu(c), w_ref[...], preferred_element_type=F32,
                         precision=lax.Precision.HIGHEST) + b_ref[...]


def _ada(c, w, b):
    bsz, d = c.shape
    n = w.shape[1]
    tn = 1536
    rows = 8
    c8 = jnp.zeros((rows, d), F32).at[:bsz].set(c)
    out = pl.pallas_call(
        _ada_kernel,
        grid=(n // tn,),
        in_specs=[pl.BlockSpec((rows, d), lambda j: (0, 0)),
                  pl.BlockSpec((d, tn), lambda j: (0, j)),
                  pl.BlockSpec((1, tn), lambda j: (0, j))],
        out_specs=pl.BlockSpec((rows, tn), lambda j: (0, j)),
        out_shape=jax.ShapeDtypeStruct((rows, n), F32),
        compiler_params=_params(("arbitrary",)),
        name="ada",
    )(c8, w, b.reshape(1, n))
    return out[:bsz]


def _inproj_kernel(x_ref, nw_ref, sh_ref, sc_ref, wa_ref, wd_ref, wz_ref, wb_ref, wal_ref, wg_ref,
                   oa_ref, od_ref, oz_ref, ob_ref, oal_ref, og_ref):
    x = x_ref[0]
    ms = jnp.mean(x * x, axis=-1, keepdims=True)
    h = x * lax.rsqrt(ms + EPS) * nw_ref[...]
    h = h * (1.0 + sc_ref[0]) + sh_ref[0]
    hb = h.astype(BF16)
    oa_ref[0] = _dot(hb, wa_ref[...]).astype(BF16)
    od_ref[0] = _dot(hb, wd_ref[...]).astype(BF16)
    oz_ref[0] = _dot(hb, wz_ref[...]).astype(BF16)
    ob_ref[0] = _dot(hb, wb_ref[...])
    oal_ref[0] = _dot(hb, wal_ref[...])
    og_ref[0] = _dot(hb, wg_ref[...]).astype(BF16)


def _inproj(x, nw, sh, sc, ws, tm):
    bsz, s, d = x.shape
    widths = [w.shape[1] for w in ws]
    dts = [BF16, BF16, BF16, F32, F32, BF16]
    tok = lambda w: pl.BlockSpec((1, tm, w), lambda b, i: (b, i, 0))
    full = lambda a: pl.BlockSpec(a.shape, lambda b, i: (0,) * a.ndim)
    vec = pl.BlockSpec((1, 1, d), lambda b, i: (b, 0, 0))
    return pl.pallas_call(
        _inproj_kernel,
        grid=(bsz, s // tm),
        in_specs=[tok(d), full(nw), vec, vec] + [full(w) for w in ws],
        out_specs=[tok(w) for w in widths],
        out_shape=[jax.ShapeDtypeStruct((bsz, s, w), dt) for w, dt in zip(widths, dts)],
        compiler_params=_params(("parallel", "parallel")),
        name="inproj",
    )(x, nw, sh, sc, *ws)


def _norm_rope(x, w, bd, cos, sa, sb):
    n = x.shape[1]
    ms = _split_dot(x * x, bd, 2)
    xn = x * lax.rsqrt(ms + EPS) * w
    return xn * cos + pltpu.roll(xn, n - 16, 1) * sa + pltpu.roll(xn, 16, 1) * sb


def _attn_prep_kernel(a_ref, cos_ref, sa_ref, sb_ref, qw_ref, kw_ref, bdq_ref, bdk_ref,
                      q_ref, kt_ref, v_ref):
    a = a_ref[0].astype(F32)
    q = a[:, :ATTN_WIDTH]
    k = a[:, ATTN_WIDTH:ATTN_WIDTH + KV_WIDTH]
    v = a[:, ATTN_WIDTH + KV_WIDTH:]
    cos, sa, sb = cos_ref[...], sa_ref[...], sb_ref[...]
    rep = ATTN_WIDTH // LANES
    tile = lambda t: jnp.concatenate([t] * rep, axis=1)
    qr = _norm_rope(q, qw_ref[...], bdq_ref[...], tile(cos), tile(sa), tile(sb))
    q_ref[0] = (qr * (HEAD_DIM ** -0.5 * LOG2E)).astype(BF16)
    kr = _norm_rope(k, kw_ref[...], bdk_ref[...], cos, sa, sb)
    kt = kr.T
    kt_ref[0, 0] = kt[:HEAD_DIM].astype(BF16)
    kt_ref[0, 1] = kt[HEAD_DIM:].astype(BF16)
    lane = lax.broadcasted_iota(jnp.int32, v.shape, 1)
    ones_col = jnp.where(lane == HEAD_DIM, 1.0, 0.0)
    v_ref[0, 0] = jnp.where(lane < HEAD_DIM, v, ones_col).astype(BF16)
    v_ref[0, 1] = jnp.where(lane < HEAD_DIM, pltpu.roll(v, HEAD_DIM, 1), ones_col).astype(BF16)


def _rope_tables(s):
    pos = jnp.arange(s)
    lane = jnp.arange(LANES)
    d = lane % HEAD_DIM
    p = d % 32
    f = (p % 16).astype(F32)
    freqs = ROPE_THETA ** (-(f * 2.0 / 32.0))
    axis_pos = jnp.where((d // 32)[None, :] == 0, (pos // GRID_W)[:, None], (pos % GRID_W)[:, None])
    ang = axis_pos.astype(F32) * freqs[None, :]
    cos, sin = jnp.cos(ang), jnp.sin(ang)
    first = (p < 16)[None, :]
    return cos, jnp.where(first, -sin, 0.0), jnp.where(first, 0.0, sin)


def _block_diag(n, blk, val):
    i = np.arange(n)
    return jnp.asarray(np.where((i[:, None] // blk) == (i[None, :] // blk), val, 0.0), BF16)


def _attn_prep(a, qw, kw, tp):
    bsz, s, wa = a.shape
    cos, sa, sb = _rope_tables(s)
    bdq = _block_diag(ATTN_WIDTH, HEAD_DIM, 1.0 / HEAD_DIM)
    bdk = _block_diag(KV_WIDTH, HEAD_DIM, 1.0 / HEAD_DIM)
    qw_t = jnp.tile(qw, N_Q_HEADS).reshape(1, ATTN_WIDTH)
    kw_t = jnp.tile(kw, N_KV_HEADS).reshape(1, KV_WIDTH)
    tab = pl.BlockSpec((tp, LANES), lambda b, i: (i, 0))
    full = lambda t: pl.BlockSpec(t.shape, lambda b, i: (0,) * t.ndim)
    return pl.pallas_call(
        _attn_prep_kernel,
        grid=(bsz, s // tp),
        in_specs=[pl.BlockSpec((1, tp, wa), lambda b, i: (b, i, 0)), tab, tab, tab,
                  full(qw_t), full(kw_t), full(bdq), full(bdk)],
        out_specs=[pl.BlockSpec((1, tp, ATTN_WIDTH), lambda b, i: (b, i, 0)),
                   pl.BlockSpec((1, N_KV_HEADS, HEAD_DIM, tp), lambda b, i: (b, 0, 0, i)),
                   pl.BlockSpec((1, N_KV_HEADS, tp, LANES), lambda b, i: (b, 0, i, 0))],
        out_shape=[jax.ShapeDtypeStruct((bsz, s, ATTN_WIDTH), BF16),
                   jax.ShapeDtypeStruct((bsz, N_KV_HEADS, HEAD_DIM, s), BF16),
                   jax.ShapeDtypeStruct((bsz, N_KV_HEADS, s, LANES), BF16)],
        compiler_params=_params(("parallel", "parallel")),
        name="attn_prep",
    )(a, cos, sa, sb, qw_t, kw_t, bdq, bdk)


def _attn_kernel(q_ref, kt_ref, v_ref, o_ref, *, tq, tk, group):
    rows = group * tq
    q = q_ref[0]
    qs = jnp.concatenate([q[:, h * HEAD_DIM:(h + 1) * HEAD_DIM] for h in range(group)], axis=0)
    nk = kt_ref.shape[3] // tk
    scores = lambda k: _dot(qs, kt_ref[0, 0, :, k * tk:(k + 1) * tk])
    m = jnp.full((rows, LANES), -jnp.inf, F32)
    acc = jnp.zeros((rows, LANES), F32)
    s_next = scores(0)
    for k in range(nk):
        s = s_next
        if k + 1 < nk:
            s_next = scores(k + 1)
        m_new = jnp.maximum(m, jnp.max(s, axis=1, keepdims=True))
        alpha = jnp.exp2(m - m_new)
        p = jnp.concatenate([jnp.exp2(s[:, t * LANES:(t + 1) * LANES] - m_new).astype(BF16)
                             for t in range(tk // LANES)], axis=1)
        acc = alpha * acc + _dot(p, v_ref[0, 0, k * tk:(k + 1) * tk, :])
        m = m_new
    o = acc[:, :HEAD_DIM] * (1.0 / acc[:, HEAD_DIM:HEAD_DIM + 1])
    o_ref[0] = jnp.concatenate([o[h * tq:(h + 1) * tq] for h in range(group)], axis=1).astype(BF16)


def _attention(q, kt, v, tq, tk):
    bsz, s, _ = q.shape
    group = N_Q_HEADS // N_KV_HEADS
    gw = group * HEAD_DIM
    return pl.pallas_call(
        functools.partial(_attn_kernel, tq=tq, tk=tk, group=group),
        grid=(bsz, N_KV_HEADS, s // tq),
        in_specs=[pl.BlockSpec((1, tq, gw), lambda b, g, i: (b, i, g)),
                  pl.BlockSpec((1, 1, HEAD_DIM, s), lambda b, g, i: (b, g, 0, 0)),
                  pl.BlockSpec((1, 1, s, LANES), lambda b, g, i: (b, g, 0, 0))],
        out_specs=pl.BlockSpec((1, tq, gw), lambda b, g, i: (b, i, g)),
        out_shape=jax.ShapeDtypeStruct((bsz, s, ATTN_WIDTH), BF16),
        compiler_params=_params(("parallel", "parallel", "parallel")),
        name="attention",
    )(q, kt, v)


def _dn_prep_kernel(cur_ref, prev_ref, next_ref, cw_ref, braw_ref, araw_ref, alog_ref, dtb_ref,
                    bd_ref, trif_ref, trib_ref, tot_ref, sel_ref, exf_ref, exb_ref,
                    kn_ref, qn_ref, vb_ref, win_ref, qg_ref, kdt_ref, gc_ref, gct_ref, beta_ref, gl_ref,
                    ext, *, ts):
    i = pl.program_id(1)
    halo = 16
    w3 = 3 * DN_WIDTH
    ext[halo:halo + ts, :] = cur_ref[0].astype(F32)
    ext[0:halo, :] = jnp.where(i > 0, prev_ref[0].astype(F32), 0.0)
    ext[halo + ts:, :] = jnp.where(i < pl.num_programs(1) - 1, next_ref[0].astype(F32), 0.0)
    cw = cw_ref[...]
    conv = jnp.zeros((ts, w3), F32)
    for j in range(CONV_WIDTH):
        off = halo - CONV_WIDTH // 2 + j
        conv = conv + ext[off:off + ts, :] * cw[j:j + 1, :]
    act = _silu(conv)
    cq, ck, cv = act[:, :DN_WIDTH], act[:, DN_WIDTH:2 * DN_WIDTH], act[:, 2 * DN_WIDTH:]
    bd = bd_ref[...]
    qn = cq * lax.rsqrt(_split_dot(cq * cq, bd, 2) + EPS) * (DN_HEAD_DIM ** -0.5)
    kn = ck * lax.rsqrt(_split_dot(ck * ck, bd, 2) + EPS)
    kn_ref[0] = kn.astype(BF16)
    qn_ref[0] = qn.astype(BF16)

    beta = jax.nn.sigmoid(braw_ref[0])
    g = -jnp.exp(alog_ref[...]) * jax.nn.softplus(araw_ref[0] + dtb_ref[...])
    gtot = _split_dot_l(tot_ref[...], g, 3)
    gl = jnp.exp(_split_dot_l(sel_ref[...], g, 3))
    for d, (tri_ref, ex_ref) in enumerate(((trif_ref, exf_ref), (trib_ref, exb_ref))):
        gc = _split_dot_l(tri_ref[...], g, 3)
        ex = ex_ref[...]
        beta_x = _split_dot(beta, ex, 2)
        eg_x = _split_dot(jnp.exp(gc), ex, 2)
        ek_x = _split_dot(jnp.exp(gtot - gc), ex, 2)
        vb_ref[d, 0] = (cv * beta_x).astype(BF16)
        win_ref[d, 0] = (kn * beta_x * eg_x).astype(BF16)
        qg_ref[d, 0] = (qn * eg_x).astype(BF16)
        kdt_ref[d, 0] = (kn * ek_x).T.astype(BF16)
        shift = (LANES - d * DN_HEADS) % LANES
        gc_d = gc if d == 0 else pltpu.roll(gc, shift, 1)
        gc_ref[d, 0] = gc_d
        gct_ref[d, 0] = gc_d.T[:DN_HEADS]
        beta_ref[d, 0] = beta if d == 0 else pltpu.roll(beta, shift, 1)
        gl_ref[d, 0] = _split_dot(gl, ex, 2)


def _dn_prep(dqkv, braw, araw, conv_w, a_log, dt_bias, ts):
    bsz, s, w3 = dqkv.shape
    nt = s // ts
    cpt = ts // CHUNK
    assert cpt == 8
    halo = 16
    hb = ts // halo
    idx = np.arange(ts)
    same = (idx[:, None] // CHUNK) == (idx[None, :] // CHUNK)
    trif = jnp.asarray(np.where(same & (idx[:, None] >= idx[None, :]), 1.0, 0.0), BF16)
    trib = jnp.asarray(np.where(same & (idx[:, None] <= idx[None, :]), 1.0, 0.0), BF16)
    tot = jnp.asarray(np.where(same, 1.0, 0.0), BF16)
    sel = jnp.asarray(np.where(np.arange(cpt)[:, None] == (idx[None, :] // CHUNK), 1.0, 0.0), BF16)
    bd = _block_diag(DN_WIDTH, DN_HEAD_DIM, 1.0)
    lane = np.arange(DN_WIDTH) // DN_HEAD_DIM
    row = np.arange(LANES)
    exf = jnp.asarray(np.where(row[:, None] == lane[None, :], 1.0, 0.0), BF16)
    exb = jnp.asarray(np.where(row[:, None] == lane[None, :] + DN_HEADS, 1.0, 0.0), BF16)
    nh2 = 2 * DN_HEADS
    alog = jnp.zeros((1, LANES), F32).at[0, :nh2].set(a_log.reshape(nh2))
    dtb = jnp.zeros((1, LANES), F32).at[0, :nh2].set(dt_bias.reshape(nh2))
    full = lambda t: pl.BlockSpec(t.shape, lambda b, i: (0,) * t.ndim)
    tok = lambda w: pl.BlockSpec((1, ts, w), lambda b, i: (b, i, 0))
    dtok = lambda w: pl.BlockSpec((2, 1, ts, w), lambda b, i: (0, b, i, 0))
    sds = jax.ShapeDtypeStruct
    return pl.pallas_call(
        functools.partial(_dn_prep_kernel, ts=ts),
        grid=(bsz, nt),
        in_specs=[tok(w3),
                  pl.BlockSpec((1, halo, w3), lambda b, i: (b, jnp.maximum(i * hb - 1, 0), 0)),
                  pl.BlockSpec((1, halo, w3), lambda b, i: (b, jnp.minimum((i + 1) * hb, s // halo - 1), 0)),
                  full(conv_w), tok(LANES), tok(LANES), full(alog), full(dtb),
                  full(bd), full(trif), full(trib), full(tot), full(sel), full(exf), full(exb)],
        out_specs=[tok(DN_WIDTH), tok(DN_WIDTH), dtok(DN_WIDTH), dtok(DN_WIDTH), dtok(DN_WIDTH),
                   pl.BlockSpec((2, 1, DN_WIDTH, ts), lambda b, i: (0, b, 0, i)),
                   dtok(LANES),
                   pl.BlockSpec((2, 1, DN_HEADS, ts), lambda b, i: (0, b, 0, i)),
                   dtok(LANES),
                   pl.BlockSpec((2, 1, cpt, DN_WIDTH), lambda b, i: (0, b, i, 0))],
        out_shape=[sds((bsz, s, DN_WIDTH), BF16), sds((bsz, s, DN_WIDTH), BF16),
                   sds((2, bsz, s, DN_WIDTH), BF16), sds((2, bsz, s, DN_WIDTH), BF16),
                   sds((2, bsz, s, DN_WIDTH), BF16), sds((2, bsz, DN_WIDTH, s), BF16),
                   sds((2, bsz, s, LANES), F32), sds((2, bsz, DN_HEADS, s), F32),
                   sds((2, bsz, s, LANES), F32), sds((2, bsz, s // CHUNK, DN_WIDTH), F32)],
        scratch_shapes=[pltpu.VMEM((ts + 2 * halo, w3), F32)],
        compiler_params=_params(("parallel", "parallel")),
        name="dn_prep",
    )(dqkv, dqkv, dqkv, conv_w, braw, araw, alog, dtb, bd, trif, trib, tot, sel, exf, exb)


def _dn_scan_kernel(kn_ref, qn_ref, vb_ref, win_ref, qg_ref, kdt_ref, gc_ref, gct_ref, beta_ref, gl_ref,
                    o_ref, state, *, rev, cpb, gl_rows):
    j = pl.program_id(1)
    nb = pl.num_programs(1)

    @pl.when(j == 0)
    def _():
        state[...] = jnp.zeros(state.shape, F32)

    jj = (nb - 1 - j) if rev else j
    ri = lax.broadcasted_iota(jnp.int32, (CHUNK, CHUNK), 0)
    ci = lax.broadcasted_iota(jnp.int32, (CHUNK, CHUNK), 1)
    incl = (ri <= ci) if rev else (ri >= ci)
    strict = (ri < ci) if rev else (ri > ci)
    eye = jnp.where(ri == ci, 1.0, 0.0)
    merge = [((ri // (2 * sz)) == (ci // (2 * sz))) & ((ri // sz) != (ci // sz))
             for sz in (1, 2, 4, 8, 16, 32)]

    order = list(range(cpb - 1, -1, -1) if rev else range(cpb))
    heads = range(DN_HEADS)
    items = [(c, h) for c in order for h in heads]
    rows = {c: slice(c * CHUNK, (c + 1) * CHUNK) for c in order}
    ls = {h: slice(h * DN_HEAD_DIM, (h + 1) * DN_HEAD_DIM) for h in heads}
    k_c = {c: kn_ref[0, rows[c], :] for c in order}
    q_c = {c: qn_ref[0, rows[c], :] for c in order}
    gc_c = {c: gc_ref[0, 0, rows[c], :] for c in order}
    gct_c = {c: gct_ref[0, 0, :, rows[c]] for c in order}
    beta_c = {c: beta_ref[0, 0, rows[c], :] for c in order}
    a_kk = {(c, h): lax.dot_general(k_c[c][:, ls[h]], k_c[c][:, ls[h]], NT_DIMS, preferred_element_type=F32)
            for c, h in items}
    a_qk = {(c, h): lax.dot_general(q_c[c][:, ls[h]], k_c[c][:, ls[h]], NT_DIMS, preferred_element_type=F32)
            for c, h in items}
    dec, lm, t = {}, {}, {}
    for c, h in items:
        diff = gc_c[c][:, h:h + 1] - gct_c[c][h:h + 1, :]
        dec[c, h] = jnp.exp(jnp.where(incl, diff, -jnp.inf))
        lm[c, h] = jnp.where(strict, a_kk[c, h] * beta_c[c][:, h:h + 1] * dec[c, h], 0.0)
        t[c, h] = eye - jnp.where(merge[0], lm[c, h], 0.0)
    for mk in merge[1:]:
        tb = {i: t[i].astype(BF16) for i in items}
        x = {i: _dot(jnp.where(mk, lm[i], 0.0).astype(BF16), tb[i]).astype(BF16) for i in items}
        t = {i: t[i] - _dot(tb[i], x[i]) for i in items}
    tb = {i: t[i].astype(BF16) for i in items}
    u = {(c, h): _dot(tb[c, h], vb_ref[0, 0, rows[c], ls[h]]) for c, h in items}
    w = {(c, h): _dot(tb[c, h], win_ref[0, 0, rows[c], ls[h]]).astype(BF16) for c, h in items}
    a_in = {i: (a_qk[i] * dec[i]).astype(BF16) for i in items}

    for c in order:
        gl_c = gl_ref[0, 0, pl.ds((jj * cpb) % gl_rows + c, 1), :]
        s_old = {h: state[h] for h in heads}
        sb = {h: s_old[h].astype(BF16) for h in heads}
        ws = {h: _dot(w[c, h], sb[h]) for h in heads}
        qs = {h: _dot(qg_ref[0, 0, rows[c], ls[h]], sb[h]) for h in heads}
        vnb = {h: (u[c, h] - ws[h]).astype(BF16) for h in heads}
        o_in = {h: _dot(a_in[c, h], vnb[h]) for h in heads}
        ds = {h: _dot(kdt_ref[0, 0, ls[h], rows[c]], vnb[h]) for h in heads}
        for h in heads:
            state[h] = s_old[h] * gl_c[:, ls[h]] + ds[h]
        o_ref[0, rows[c], :] = jnp.concatenate([qs[h] + o_in[h] for h in heads], axis=1)


def _dn_scan(prep, rev, tsb, gl_rows):
    kn, qn, vb, win, qg, kdt, gc, gct, beta, gl = prep
    bsz, s, _ = kn.shape
    nb = s // tsb
    cpb = tsb // CHUNK
    d = 1 if rev else 0
    blk = (lambda j: nb - 1 - j) if rev else (lambda j: j)
    tok = pl.BlockSpec((1, tsb, DN_WIDTH), lambda b, j: (b, blk(j), 0))
    dtok = lambda w: pl.BlockSpec((1, 1, tsb, w), lambda b, j: (d, b, blk(j), 0))
    return pl.pallas_call(
        functools.partial(_dn_scan_kernel, rev=rev, cpb=cpb, gl_rows=gl_rows),
        grid=(bsz, nb),
        in_specs=[tok, tok, dtok(DN_WIDTH), dtok(DN_WIDTH), dtok(DN_WIDTH),
                  pl.BlockSpec((1, 1, DN_WIDTH, tsb), lambda b, j: (d, b, 0, blk(j))),
                  dtok(LANES),
                  pl.BlockSpec((1, 1, DN_HEADS, tsb), lambda b, j: (d, b, 0, blk(j))),
                  dtok(LANES),
                  pl.BlockSpec((1, 1, gl_rows, DN_WIDTH), lambda b, j: (d, b, (blk(j) * cpb) // gl_rows, 0))],
        out_specs=tok,
        out_shape=jax.ShapeDtypeStruct((bsz, s, DN_WIDTH), F32),
        scratch_shapes=[pltpu.VMEM((DN_HEADS, DN_HEAD_DIM, DN_HEAD_DIM), F32)],
        compiler_params=_params(("parallel", "arbitrary")),
        name="dn_scan_bwd" if rev else "dn_scan_fwd",
    )(kn, qn, vb, win, qg, kdt, gc, gct, beta, gl)


def _post_kernel(attn_ref, of_ref, ob_ref, dz_ref, g_ref, x_ref, gt1_ref, sh2_ref, sc2_ref,
                 dnw_ref, n2w_ref, bd_ref, wau_ref, wdu_ref, wo_ref, wr_ref,
                 x1_ref, h2_ref, aff_ref, afft_ref):
    o = of_ref[0] + ob_ref[0]
    ms = _split_dot(o * o, bd_ref[...], 2)
    dn = o * lax.rsqrt(ms + EPS) * dnw_ref[...] * _silu(dz_ref[0].astype(F32))
    gates = jax.nn.sigmoid(g_ref[0].astype(F32))
    d = x_ref.shape[2]
    merged = (gates[:, :d] * _dot(attn_ref[0], wau_ref[...])
              + gates[:, d:] * _dot(dn.astype(BF16), wdu_ref[...]))
    x1 = x_ref[0] + gt1_ref[0] * _dot(merged.astype(BF16), wo_ref[...])
    x1_ref[0] = x1
    ms2 = jnp.mean(x1 * x1, axis=-1, keepdims=True)
    h2 = x1 * lax.rsqrt(ms2 + EPS) * n2w_ref[...]
    h2 = h2 * (1.0 + sc2_ref[0]) + sh2_ref[0]
    h2_ref[0] = h2.astype(BF16)
    logits = jnp.dot(h2, wr_ref[...], preferred_element_type=F32, precision=lax.Precision.HIGHEST)
    lane = lax.broadcasted_iota(jnp.int32, logits.shape, 1)
    logits = jnp.where(lane < N_EXPERTS, logits, -jnp.inf)
    e = jnp.exp(logits - jnp.max(logits, axis=1, keepdims=True))
    aff = e / jnp.sum(e, axis=1, keepdims=True)
    aff_ref[0] = aff[:, :N_EXPERTS]
    afft_ref[0] = aff.T[:N_EXPERTS]


def _post(attn, o_f, o_b, dz, graw, x, gt1, sh2, sc2, dnw, n2w, wau, wdu, wo, wr, tm):
    bsz, s, d = x.shape
    bd = _block_diag(DN_WIDTH, DN_HEAD_DIM, 1.0 / DN_HEAD_DIM)
    dnw_t = jnp.tile(dnw, DN_HEADS).reshape(1, DN_WIDTH)
    wr_p = jnp.zeros((d, LANES), F32).at[:, :N_EXPERTS].set(wr)
    tok = lambda w: pl.BlockSpec((1, tm, w), lambda b, i: (b, i, 0))
    full = lambda t: pl.BlockSpec(t.shape, lambda b, i: (0,) * t.ndim)
    vec = pl.BlockSpec((1, 1, d), lambda b, i: (b, 0, 0))
    sds = jax.ShapeDtypeStruct
    return pl.pallas_call(
        _post_kernel,
        grid=(bsz, s // tm),
        in_specs=[tok(ATTN_WIDTH), tok(DN_WIDTH), tok(DN_WIDTH), tok(DN_WIDTH), tok(2 * d), tok(d),
                  vec, vec, vec, full(dnw_t), full(n2w), full(bd), full(wau), full(wdu), full(wo), full(wr_p)],
        out_specs=[tok(d), tok(d), tok(N_EXPERTS),
                   pl.BlockSpec((1, N_EXPERTS, tm), lambda b, i: (b, 0, i))],
        out_shape=[sds((bsz, s, d), F32), sds((bsz, s, d), BF16), sds((bsz, s, N_EXPERTS), F32),
                   sds((bsz, N_EXPERTS, s), F32)],
        compiler_params=_params(("parallel", "parallel")),
        name="post_mixer",
    )(attn, o_f, o_b, dz, graw, x, gt1, sh2, sc2, dnw_t, n2w, bd, wau, wdu, wo, wr_p)


def _topk_kernel(aff_ref, tri_ref, code_ref, *, cap, s):
    a = aff_ref[0]
    ne = a.shape[0]
    count = lambda m: jnp.sum(jnp.where(m, 1.0, 0.0), axis=1, keepdims=True)

    def vbody(i, lo):
        cand = lo | (jnp.int32(1) << (30 - i))
        return jnp.where(count(a >= pltpu.bitcast(cand, F32)) >= cap, cand, lo)

    lo_bits = lax.fori_loop(0, 31, vbody, jnp.zeros((ne, 1), jnp.int32))
    lo = pltpu.bitcast(lo_bits, F32)
    hi = pltpu.bitcast(lo_bits + 1, F32)

    def rbody(i, lh):
        lo, hi = lh
        mid = 0.5 * (lo + hi)
        ok = count(a >= mid) >= cap
        return jnp.where(ok, mid, lo), jnp.where(ok, hi, mid)

    lo, hi = lax.fori_loop(0, 32, rbody, (lo, hi))
    gt = a >= hi
    eq = (a >= lo) & jnp.logical_not(gt)
    need = cap - count(gt)
    idx = lax.broadcasted_iota(jnp.int32, a.shape, 1)
    nbits = int(np.log2(s))

    def ibody(i, x):
        cand = x | (jnp.int32(1) << (nbits - 1 - i))
        return jnp.where(count(eq & (idx < cand)) < need, cand, x)

    last = lax.fori_loop(0, nbits, ibody, jnp.zeros((ne, 1), jnp.int32))
    sel = gt | (eq & (idx <= last))
    self32 = jnp.where(sel, 1.0, 0.0)
    tri = tri_ref[...]
    carry = jnp.zeros((ne, 1), F32)
    for t in range(s // LANES):
        seg = self32[:, t * LANES:(t + 1) * LANES]
        inc = _dot(seg.astype(BF16), tri)
        pos = (inc - seg + carry).astype(jnp.int32)
        code_ref[0, :, t * LANES:(t + 1) * LANES] = pos * 2 + seg.astype(jnp.int32)
        carry = carry + inc[:, LANES - 1:LANES]


def _topk(afft, cap):
    bsz, ne, s = afft.shape
    i = np.arange(LANES)
    tri = jnp.asarray(np.where(i[:, None] <= i[None, :], 1.0, 0.0), BF16)
    return pl.pallas_call(
        functools.partial(_topk_kernel, cap=cap, s=s),
        grid=(bsz,),
        in_specs=[pl.BlockSpec((1, ne, s), lambda b: (b, 0, 0)),
                  pl.BlockSpec((LANES, LANES), lambda b: (0, 0))],
        out_specs=pl.BlockSpec((1, ne, s), lambda b: (b, 0, 0)),
        out_shape=jax.ShapeDtypeStruct((bsz, ne, s), jnp.int32),
        compiler_params=_params(("parallel",)),
        name="topk",
    )(afft, tri)


def _moe_ffn_kernel(starts_ref, code_ref, aff_ref, h2_ref, wg_ref, wu_ref, wd_ref, y_ref, xe, gacc,
                    *, tb, win, cap, nj, nsub, spt):
    b, e, j = pl.program_id(0), pl.program_id(1), pl.program_id(2)

    @pl.when(j == 0)
    def _():
        xe[...] = jnp.zeros(xe.shape, F32)
        gacc[...] = jnp.zeros(gacc.shape, F32)

    code = code_ref[0, 0, 0]
    aff = aff_ref[0, 0, 0]
    base = ((b * pl.num_programs(1) + e) * nj + j) * nsub
    a = [pl.multiple_of(starts_ref[(base + k) * spt] // 16 * 16, 16) for k in range(nsub)]
    hot = []
    for k in range(nsub):
        r = lax.broadcasted_iota(jnp.int32, (win, tb), 0) + a[k]
        hot.append(code[:, k * tb:(k + 1) * tb] == 2 * r + 1)
    rows = [_dot(jnp.where(hot[k], 1.0, 0.0).astype(BF16), h2_ref[0, k * tb:(k + 1) * tb, :]) for k in range(nsub)]
    gates = [jnp.sum(jnp.where(hot[k], aff[:, k * tb:(k + 1) * tb], 0.0), axis=1, keepdims=True)
             for k in range(nsub)]
    for k in range(nsub):
        xe[pl.ds(a[k], win), :] = xe[pl.ds(a[k], win), :] + rows[k]
        gacc[pl.ds(a[k], win), :] = gacc[pl.ds(a[k], win), :] + gates[k]

    @pl.when(j == nj - 1)
    def _():
        xb = xe[0:cap, :].astype(BF16)
        hg = _dot(xb, wg_ref[0, 0].astype(BF16))
        hu = _dot(xb, wu_ref[0, 0].astype(BF16))
        act = (_silu(hg) * hu).astype(BF16)
        y_ref[0, 0, 0:cap, :] = (_dot(act, wd_ref[0, 0].astype(BF16)) * gacc[0:cap, :]).astype(BF16)
        y_ref[0, 0, cap:, :] = jnp.zeros((y_ref.shape[2] - cap, y_ref.shape[3]), BF16)


def _moe_ffn(starts, code, afft, h2, w_gate, w_up, w_down, cap, tb, win, capp, spt):
    bsz, s, d = h2.shape
    ne = code.shape[1]
    nsub = 4
    tstep = nsub * tb
    nj = s // tstep
    f = w_gate.shape[-1]
    row = pl.BlockSpec((1, 1, 1, 1, tstep), lambda b, e, j, st: (b, e, j, 0, 0))
    wspec = lambda r, c: pl.BlockSpec((1, 1, r, c), lambda b, e, j, st: (0, e, 0, 0))
    return pl.pallas_call(
        functools.partial(_moe_ffn_kernel, tb=tb, win=win, cap=cap, nj=nj, nsub=nsub, spt=spt),
        grid_spec=pltpu.PrefetchScalarGridSpec(
            num_scalar_prefetch=1,
            grid=(bsz, ne, nj),
            in_specs=[row, row,
                      pl.BlockSpec((1, tstep, d), lambda b, e, j, st: (b, j, 0)),
                      wspec(d, f), wspec(d, f), wspec(f, d)],
            out_specs=pl.BlockSpec((1, 1, capp, d), lambda b, e, j, st: (b, e, 0, 0)),
            scratch_shapes=[pltpu.VMEM((capp, d), F32), pltpu.VMEM((capp, 1), F32)]),
        out_shape=jax.ShapeDtypeStruct((bsz, ne, capp, d), BF16),
        compiler_params=_params(("parallel", "parallel", "arbitrary")),
        name="moe_ffn",
    )(starts, code.reshape(bsz, ne, nj, 1, tstep), afft.reshape(bsz, ne, nj, 1, tstep), h2, w_gate, w_up, w_down)


TN_DIMS = (((0,), (0,)), ((), ()))


def _moe_scatter_kernel(starts_ref, code_ref, *refs, ne, tb, sub, swin, ns):
    y_refs = refs[:ne]
    x1_ref, gt2_ref, o_ref = refs[ne:]
    b, j = pl.program_id(0), pl.program_id(1)
    spb = tb // sub
    for k in range(spb):
        acc = None
        for e in range(ne):
            first = (b * ne + e) * ns + j * spb
            a_blk = starts_ref[first] // 16 * 16
            a_sub = starts_ref[first + k] // 16 * 16
            off = pl.multiple_of(a_sub - a_blk, 16)
            r = lax.broadcasted_iota(jnp.int32, (swin, sub), 0) + a_sub
            hot = code_ref[0, e:e + 1, k * sub:(k + 1) * sub] == 2 * r + 1
            z = lax.dot_general(jnp.where(hot, 1.0, 0.0).astype(BF16), y_refs[e][pl.ds(off, swin), :], TN_DIMS,
                                preferred_element_type=F32)
            acc = z if acc is None else acc + z
        rows = slice(k * sub, (k + 1) * sub)
        o_ref[0, rows, :] = x1_ref[0, rows, :] + gt2_ref[0] * acc


def _moe_scatter(starts, code, y, x1, gt2, tb, win, sub):
    bsz, s, d = x1.shape
    ne = code.shape[1]
    nj = s // tb
    ns = s // sub
    spb = tb // sub

    def window(e):
        return pl.BlockSpec((pl.squeezed, pl.squeezed, pl.Element(win), pl.Element(d)),
                            lambda b, j, st: (b, e, pl.multiple_of(st[(b * ne + e) * ns + j * spb] // 16 * 16, 16), 0))

    return pl.pallas_call(
        functools.partial(_moe_scatter_kernel, ne=ne, tb=tb, sub=sub, swin=sub + 16, ns=ns),
        grid_spec=pltpu.PrefetchScalarGridSpec(
            num_scalar_prefetch=1,
            grid=(bsz, nj),
            in_specs=[pl.BlockSpec((1, ne, tb), lambda b, j, st: (b, 0, j))]
                     + [window(e) for e in range(ne)]
                     + [pl.BlockSpec((1, tb, d), lambda b, j, st: (b, j, 0)),
                        pl.BlockSpec((1, 1, d), lambda b, j, st: (b, 0, 0))],
            out_specs=pl.BlockSpec((1, tb, d), lambda b, j, st: (b, j, 0))),
        out_shape=jax.ShapeDtypeStruct((bsz, s, d), F32),
        compiler_params=_params(("parallel", "parallel")),
        name="moe_scatter",
    )(starts, code, *([y] * ne), x1, gt2)


def _tile(s, pref):
    t = pref
    while s % t:
        t //= 2
    return t


def _layer(x, c, w_ada, b_ada, norm1_w, w_in, q_norm_w, k_norm_w, conv_w, a_log, dt_bias, dn_norm_w,
           w_attn_up, w_dn_up, w_o, norm2_w, w_router, w_gate, w_up, w_down):
    bsz, s, d = x.shape
    mod = _ada(c, w_ada, b_ada)
    sh1, sc1, gt1, sh2, sc2, gt2 = [m.reshape(bsz, 1, d) for m in jnp.split(mod, 6, axis=-1)]

    o0 = ATTN_WIDTH + 2 * KV_WIDTH
    o1 = o0 + 3 * DN_WIDTH
    o2 = o1 + DN_WIDTH
    o3 = o2 + 2 * DN_HEADS
    o4 = o3 + 2 * DN_HEADS
    pad = lambda w: jnp.zeros((d, LANES), w.dtype).at[:, :w.shape[1]].set(w)
    ws = [w_in[:, :o0], w_in[:, o0:o1], w_in[:, o1:o2], pad(w_in[:, o2:o3]), pad(w_in[:, o3:o4]), w_in[:, o4:]]
    ws = [w.astype(BF16) for w in ws]
    a_qkv, dqkv, dz, braw, araw, graw = _inproj(x, norm1_w.reshape(1, d), sh1, sc1, ws, _tile(s, 512))

    q, kt, v = _attn_prep(a_qkv, q_norm_w, k_norm_w, _tile(s, 512))
    attn = _attention(q, kt, v, _tile(s, 256), _tile(s, 1024))

    ts = 512
    prep = _dn_prep(dqkv, braw, araw, conv_w, a_log, dt_bias, ts)
    o_f = _dn_scan(prep, False, 2 * CHUNK, ts // CHUNK)
    o_b = _dn_scan(prep, True, 2 * CHUNK, ts // CHUNK)

    x1, h2, affc, afft = _post(attn, o_f, o_b, dz, graw, x, gt1, sh2, sc2, dn_norm_w, norm2_w.reshape(1, d),
                               w_attn_up.astype(BF16), w_dn_up.astype(BF16), w_o.astype(BF16), w_router,
                               _tile(s, 512))

    cap = CAPACITY_FACTOR * s // N_EXPERTS
    tb, sub = 256, 128
    win = tb + 16
    code = _topk(afft, cap)
    starts = (code[:, :, ::sub] >> 1).reshape(-1)
    y = _moe_ffn(starts, code, afft, h2, w_gate[None], w_up[None], w_down[None], cap, tb, win, cap + win, tb // sub)
    return _moe_scatter(starts, code, y, x1, gt2, tb, win, sub)


def kernel(x, c, w_ada, b_ada, norm1_w, w_in, q_norm_w, k_norm_w, conv_w, a_log, dt_bias, dn_norm_w,
           w_attn_up, w_dn_up, w_o, norm2_w, w_router, w_gate, w_up, w_down):
    depth = w_ada.shape[0]
    for l in range(depth):
        x = _layer(x, c, w_ada[l], b_ada[l], norm1_w[l], w_in[l], q_norm_w[l], k_norm_w[l], conv_w[l],
                   a_log[l], dt_bias[l], dn_norm_w[l], w_attn_up[l], w_dn_up[l], w_o[l], norm2_w[l],
                   w_router[l], w_gate[l], w_up[l], w_down[l])
    return x
```

```python
import functools

import numpy as np
import jax
import jax.numpy as jnp
from jax import lax
from jax.experimental import pallas as pl
from jax.experimental.pallas import tpu as pltpu

F32 = jnp.float32
BF16 = jnp.bfloat16

GRID_W = 64
N_Q_HEADS = 8
N_KV_HEADS = 2
HEAD_DIM = 64
ATTN_WIDTH = N_Q_HEADS * HEAD_DIM
KV_WIDTH = N_KV_HEADS * HEAD_DIM
ROPE_THETA = 10000.0
DN_HEADS = 8
DN_HEAD_DIM = 64
DN_WIDTH = DN_HEADS * DN_HEAD_DIM
CONV_WIDTH = 5
CHUNK = 64
N_EXPERTS = 16
CAPACITY_FACTOR = 2
EPS = 1e-6
LOG2E = 1.4426950408889634
LANES = 128
VMEM_LIMIT = 56 * 1024 * 1024

NT_DIMS = (((1,), (1,)), ((), ()))


def _params(sem):
    return pltpu.CompilerParams(dimension_semantics=sem, vmem_limit_bytes=VMEM_LIMIT)


def _dot(a, b):
    return jnp.dot(a, b, preferred_element_type=F32)


def _split_dot(x, m, parts):
    acc = None
    r = x
    for i in range(parts):
        h = r.astype(BF16)
        d = _dot(h, m)
        acc = d if acc is None else acc + d
        if i + 1 < parts:
            r = r - h.astype(F32)
    return acc


def _split_dot_l(m, x, parts):
    acc = None
    r = x
    for i in range(parts):
        h = r.astype(BF16)
        d = _dot(m, h)
        acc = d if acc is None else acc + d
        if i + 1 < parts:
            r = r - h.astype(F32)
    return acc


def _silu(x):
    return x * jax.nn.sigmoid(x)


def _ada_kernel(c_ref, w_ref, b_ref, o_ref):
    c = c_ref[...]
    o_ref[...] = jnp.dot(_silu(c), w_ref[...], preferred_element_type=F32,
                         precision=lax.Precision.HIGHEST) + b_ref[...]


def _ada(c, w, b):
    bsz, d = c.shape
    n = w.shape[1]
    tn = 1536
    rows = 8
    c8 = jnp.zeros((rows, d), F32).at[:bsz].set(c)
    out = pl.pallas_call(
        _ada_kernel,
        grid=(n // tn,),
        in_specs=[pl.BlockSpec((rows, d), lambda j: (0, 0)),
                  pl.BlockSpec((d, tn), lambda j: (0, j)),
                  pl.BlockSpec((1, tn), lambda j: (0, j))],
        out_specs=pl.BlockSpec((rows, tn), lambda j: (0, j)),
        out_shape=jax.ShapeDtypeStruct((rows, n), F32),
        compiler_params=_params(("arbitrary",)),
        name="ada",
    )(c8, w, b.reshape(1, n))
    return out[:bsz]


def _inproj_kernel(x_ref, nw_ref, sh_ref, sc_ref, wa_ref, wd_ref, wz_ref, wb_ref, wal_ref, wg_ref,
                   oa_ref, od_ref, oz_ref, ob_ref, oal_ref, og_ref):
    x = x_ref[0]
    ms = jnp.mean(x * x, axis=-1, keepdims=True)
    h = x * lax.rsqrt(ms + EPS) * nw_ref[...]
    h = h * (1.0 + sc_ref[0]) + sh_ref[0]
    hb = h.astype(BF16)
    oa_ref[0] = _dot(hb, wa_ref[...]).astype(BF16)
    od_ref[0] = _dot(hb, wd_ref[...]).astype(BF16)
    oz_ref[0] = _dot(hb, wz_ref[...]).astype(BF16)
    ob_ref[0] = _dot(hb, wb_ref[...])
    oal_ref[0] = _dot(hb, wal_ref[...])
    og_ref[0] = _dot(hb, wg_ref[...]).astype(BF16)


def _inproj(x, nw, sh, sc, ws, tm):
    bsz, s, d = x.shape
    widths = [w.shape[1] for w in ws]
    dts = [BF16, BF16, BF16, F32, F32, BF16]
    tok = lambda w: pl.BlockSpec((1, tm, w), lambda b, i: (b, i, 0))
    full = lambda a: pl.BlockSpec(a.shape, lambda b, i: (0,) * a.ndim)
    vec = pl.BlockSpec((1, 1, d), lambda b, i: (b, 0, 0))
    return pl.pallas_call(
        _inproj_kernel,
        grid=(bsz, s // tm),
        in_specs=[tok(d), full(nw), vec, vec] + [full(w) for w in ws],
        out_specs=[tok(w) for w in widths],
        out_shape=[jax.ShapeDtypeStruct((bsz, s, w), dt) for w, dt in zip(widths, dts)],
        compiler_params=_params(("parallel", "parallel")),
        name="inproj",
    )(x, nw, sh, sc, *ws)


def _norm_rope(x, w, bd, cos, sa, sb):
    n = x.shape[1]
    ms = _split_dot(x * x, bd, 2)
    xn = x * lax.rsqrt(ms + EPS) * w
    return xn * cos + pltpu.roll(xn, n - 16, 1) * sa + pltpu.roll(xn, 16, 1) * sb


def _attn_prep_kernel(a_ref, cos_ref, sa_ref, sb_ref, qw_ref, kw_ref, bdq_ref, bdk_ref,
                      q_ref, kt_ref, v_ref):
    a = a_ref[0].astype(F32)
    q = a[:, :ATTN_WIDTH]
    k = a[:, ATTN_WIDTH:ATTN_WIDTH + KV_WIDTH]
    v = a[:, ATTN_WIDTH + KV_WIDTH:]
    cos, sa, sb = cos_ref[...], sa_ref[...], sb_ref[...]
    rep = ATTN_WIDTH // LANES
    tile = lambda t: jnp.concatenate([t] * rep, axis=1)
    qr = _norm_rope(q, qw_ref[...], bdq_ref[...], tile(cos), tile(sa), tile(sb))
    q_ref[0] = (qr * (HEAD_DIM ** -0.5 * LOG2E)).astype(BF16)
    kr = _norm_rope(k, kw_ref[...], bdk_ref[...], cos, sa, sb)
    kt = kr.T
    kt_ref[0, 0] = kt[:HEAD_DIM].astype(BF16)
    kt_ref[0, 1] = kt[HEAD_DIM:].astype(BF16)
    lane = lax.broadcasted_iota(jnp.int32, v.shape, 1)
    ones_col = jnp.where(lane == HEAD_DIM, 1.0, 0.0)
    v_ref[0, 0] = jnp.where(lane < HEAD_DIM, v, ones_col).astype(BF16)
    v_ref[0, 1] = jnp.where(lane < HEAD_DIM, pltpu.roll(v, HEAD_DIM, 1), ones_col).astype(BF16)


def _rope_tables(s):
    pos = jnp.arange(s)
    lane = jnp.arange(LANES)
    d = lane % HEAD_DIM
    p = d % 32
    f = (p % 16).astype(F32)
    freqs = ROPE_THETA ** (-(f * 2.0 / 32.0))
    axis_pos = jnp.where((d // 32)[None, :] == 0, (pos // GRID_W)[:, None], (pos % GRID_W)[:, None])
    ang = axis_pos.astype(F32) * freqs[None, :]
    cos, sin = jnp.cos(ang), jnp.sin(ang)
    first = (p < 16)[None, :]
    return cos, jnp.where(first, -sin, 0.0), jnp.where(first, 0.0, sin)


def _block_diag(n, blk, val):
    i = np.arange(n)
    return jnp.asarray(np.where((i[:, None] // blk) == (i[None, :] // blk), val, 0.0), BF16)


def _attn_prep(a, qw, kw, tp):
    bsz, s, wa = a.shape
    cos, sa, sb = _rope_tables(s)
    bdq = _block_diag(ATTN_WIDTH, HEAD_DIM, 1.0 / HEAD_DIM)
    bdk = _block_diag(KV_WIDTH, HEAD_DIM, 1.0 / HEAD_DIM)
    qw_t = jnp.tile(qw, N_Q_HEADS).reshape(1, ATTN_WIDTH)
    kw_t = jnp.tile(kw, N_KV_HEADS).reshape(1, KV_WIDTH)
    tab = pl.BlockSpec((tp, LANES), lambda b, i: (i, 0))
    full = lambda t: pl.BlockSpec(t.shape, lambda b, i: (0,) * t.ndim)
    return pl.pallas_call(
        _attn_prep_kernel,
        grid=(bsz, s // tp),
        in_specs=[pl.BlockSpec((1, tp, wa), lambda b, i: (b, i, 0)), tab, tab, tab,
                  full(qw_t), full(kw_t), full(bdq), full(bdk)],
        out_specs=[pl.BlockSpec((1, tp, ATTN_WIDTH), lambda b, i: (b, i, 0)),
                   pl.BlockSpec((1, N_KV_HEADS, HEAD_DIM, tp), lambda b, i: (b, 0, 0, i)),
                   pl.BlockSpec((1, N_KV_HEADS, tp, LANES), lambda b, i: (b, 0, i, 0))],
        out_shape=[jax.ShapeDtypeStruct((bsz, s, ATTN_WIDTH), BF16),
                   jax.ShapeDtypeStruct((bsz, N_KV_HEADS, HEAD_DIM, s), BF16),
                   jax.ShapeDtypeStruct((bsz, N_KV_HEADS, s, LANES), BF16)],
        compiler_params=_params(("parallel", "parallel")),
        name="attn_prep",
    )(a, cos, sa, sb, qw_t, kw_t, bdq, bdk)


def _attn_kernel(q_ref, kt_ref, v_ref, o_ref, *, tq, tk, group):
    rows = group * tq
    q = q_ref[0]
    qs = jnp.concatenate([q[:, h * HEAD_DIM:(h + 1) * HEAD_DIM] for h in range(group)], axis=0)
    nk = kt_ref.shape[3] // tk
    scores = lambda k: _dot(qs, kt_ref[0, 0, :, k * tk:(k + 1) * tk])
    m = jnp.full((rows, LANES), -jnp.inf, F32)
    acc = jnp.zeros((rows, LANES), F32)
    s_next = scores(0)
    for k in range(nk):
        s = s_next
        if k + 1 < nk:
            s_next = scores(k + 1)
        m_new = jnp.maximum(m, jnp.max(s, axis=1, keepdims=True))
        alpha = jnp.exp2(m - m_new)
        p = jnp.concatenate([jnp.exp2(s[:, t * LANES:(t + 1) * LANES] - m_new).astype(BF16)
                             for t in range(tk // LANES)], axis=1)
        acc = alpha * acc + _dot(p, v_ref[0, 0, k * tk:(k + 1) * tk, :])
        m = m_new
    o = acc[:, :HEAD_DIM] * (1.0 / acc[:, HEAD_DIM:HEAD_DIM + 1])
    o_ref[0] = jnp.concatenate([o[h * tq:(h + 1) * tq] for h in range(group)], axis=1).astype(BF16)


def _attention(q, kt, v, tq, tk):
    bsz, s, _ = q.shape
    group = N_Q_HEADS // N_KV_HEADS
    gw = group * HEAD_DIM
    return pl.pallas_call(
        functools.partial(_attn_kernel, tq=tq, tk=tk, group=group),
        grid=(bsz, N_KV_HEADS, s // tq),
        in_specs=[pl.BlockSpec((1, tq, gw), lambda b, g, i: (b, i, g)),
                  pl.BlockSpec((1, 1, HEAD_DIM, s), lambda b, g, i: (b, g, 0, 0)),
                  pl.BlockSpec((1, 1, s, LANES), lambda b, g, i: (b, g, 0, 0))],
        out_specs=pl.BlockSpec((1, tq, gw), lambda b, g, i: (b, i, g)),
        out_shape=jax.ShapeDtypeStruct((bsz, s, ATTN_WIDTH), BF16),
        compiler_params=_params(("parallel", "parallel", "parallel")),
        name="attention",
    )(q, kt, v)


def _dn_prep_kernel(cur_ref, prev_ref, next_ref, cw_ref, braw_ref, araw_ref, alog_ref, dtb_ref,
                    bd_ref, trif_ref, trib_ref, tot_ref, sel_ref, exf_ref, exb_ref,
                    kn_ref, qn_ref, vb_ref, win_ref, qg_ref, kdt_ref, gc_ref, gct_ref, beta_ref, gl_ref,
                    *, ts):
    i = pl.program_id(1)
    halo = prev_ref.shape[1]
    ext = jnp.concatenate([jnp.where(i > 0, prev_ref[0].astype(F32), 0.0),
                           cur_ref[0].astype(F32),
                           jnp.where(i < pl.num_programs(1) - 1, next_ref[0].astype(F32), 0.0)], axis=0)
    cw = cw_ref[...]
    conv = None
    for j in range(CONV_WIDTH):
        shift = (CONV_WIDTH // 2 - j) % ext.shape[0]
        tap = (pltpu.roll(ext, shift, 0) if shift else ext)[halo:halo + ts] * cw[j:j + 1, :]
        conv = tap if conv is None else conv + tap
    act = _silu(conv)
    cq, ck, cv = act[:, :DN_WIDTH], act[:, DN_WIDTH:2 * DN_WIDTH], act[:, 2 * DN_WIDTH:]
    bd = bd_ref[...]
    qn = cq * lax.rsqrt(_split_dot(cq * cq, bd, 2) + EPS) * (DN_HEAD_DIM ** -0.5)
    kn = ck * lax.rsqrt(_split_dot(ck * ck, bd, 2) + EPS)
    kn_ref[0] = kn.astype(BF16)
    qn_ref[0] = qn.astype(BF16)

    beta = jax.nn.sigmoid(braw_ref[0])
    g = -jnp.exp(alog_ref[...]) * jax.nn.softplus(araw_ref[0] + dtb_ref[...])
    gtot = _split_dot_l(tot_ref[...], g, 3)
    gl = jnp.exp(_split_dot_l(sel_ref[...], g, 3))
    for d, (tri_ref, ex_ref) in enumerate(((trif_ref, exf_ref), (trib_ref, exb_ref))):
        gc = _split_dot_l(tri_ref[...], g, 3)
        ex = ex_ref[...]
        beta_x = _split_dot(beta, ex, 2)
        eg_x = _split_dot(jnp.exp(gc), ex, 2)
        ek_x = _split_dot(jnp.exp(gtot - gc), ex, 2)
        vb_ref[d, 0] = (cv * beta_x).astype(BF16)
        win_ref[d, 0] = (kn * beta_x * eg_x).astype(BF16)
        qg_ref[d, 0] = (qn * eg_x).astype(BF16)
        kdt_ref[d, 0] = (kn * ek_x).T.astype(BF16)
        shift = (LANES - d * DN_HEADS) % LANES
        gc_d = gc if d == 0 else pltpu.roll(gc, shift, 1)
        gc_ref[d, 0] = gc_d
        gct_ref[d, 0] = gc_d.T[:DN_HEADS]
        beta_ref[d, 0] = beta if d == 0 else pltpu.roll(beta, shift, 1)
        gl_ref[d, 0] = _split_dot(gl, ex, 2)


def _dn_prep(dqkv, braw, araw, conv_w, a_log, dt_bias, ts):
    bsz, s, w3 = dqkv.shape
    nt = s // ts
    cpt = ts // CHUNK
    assert cpt == 8
    halo = 16
    hb = ts // halo
    idx = np.arange(ts)
    same = (idx[:, None] // CHUNK) == (idx[None, :] // CHUNK)
    trif = jnp.asarray(np.where(same & (idx[:, None] >= idx[None, :]), 1.0, 0.0), BF16)
    trib = jnp.asarray(np.where(same & (idx[:, None] <= idx[None, :]), 1.0, 0.0), BF16)
    tot = jnp.asarray(np.where(same, 1.0, 0.0), BF16)
    sel = jnp.asarray(np.where(np.arange(cpt)[:, None] == (idx[None, :] // CHUNK), 1.0, 0.0), BF16)
    bd = _block_diag(DN_WIDTH, DN_HEAD_DIM, 1.0)
    lane = np.arange(DN_WIDTH) // DN_HEAD_DIM
    row = np.arange(LANES)
    exf = jnp.asarray(np.where(row[:, None] == lane[None, :], 1.0, 0.0), BF16)
    exb = jnp.asarray(np.where(row[:, None] == lane[None, :] + DN_HEADS, 1.0, 0.0), BF16)
    nh2 = 2 * DN_HEADS
    alog = jnp.zeros((1, LANES), F32).at[0, :nh2].set(a_log.reshape(nh2))
    dtb = jnp.zeros((1, LANES), F32).at[0, :nh2].set(dt_bias.reshape(nh2))
    full = lambda t: pl.BlockSpec(t.shape, lambda b, i: (0,) * t.ndim)
    tok = lambda w: pl.BlockSpec((1, ts, w), lambda b, i: (b, i, 0))
    dtok = lambda w: pl.BlockSpec((2, 1, ts, w), lambda b, i: (0, b, i, 0))
    sds = jax.ShapeDtypeStruct
    return pl.pallas_call(
        functools.partial(_dn_prep_kernel, ts=ts),
        grid=(bsz, nt),
        in_specs=[tok(w3),
                  pl.BlockSpec((1, halo, w3), lambda b, i: (b, jnp.maximum(i * hb - 1, 0), 0)),
                  pl.BlockSpec((1, halo, w3), lambda b, i: (b, jnp.minimum((i + 1) * hb, s // halo - 1), 0)),
                  full(conv_w), tok(LANES), tok(LANES), full(alog), full(dtb),
                  full(bd), full(trif), full(trib), full(tot), full(sel), full(exf), full(exb)],
        out_specs=[tok(DN_WIDTH), tok(DN_WIDTH), dtok(DN_WIDTH), dtok(DN_WIDTH), dtok(DN_WIDTH),
                   pl.BlockSpec((2, 1, DN_WIDTH, ts), lambda b, i: (0, b, 0, i)),
                   dtok(LANES),
                   pl.BlockSpec((2, 1, DN_HEADS, ts), lambda b, i: (0, b, 0, i)),
                   dtok(LANES),
                   pl.BlockSpec((2, 1, cpt, DN_WIDTH), lambda b, i: (0, b, i, 0))],
        out_shape=[sds((bsz, s, DN_WIDTH), BF16), sds((bsz, s, DN_WIDTH), BF16),
                   sds((2, bsz, s, DN_WIDTH), BF16), sds((2, bsz, s, DN_WIDTH), BF16),
                   sds((2, bsz, s, DN_WIDTH), BF16), sds((2, bsz, DN_WIDTH, s), BF16),
                   sds((2, bsz, s, LANES), F32), sds((2, bsz, DN_HEADS, s), F32),
                   sds((2, bsz, s, LANES), F32), sds((2, bsz, s // CHUNK, DN_WIDTH), F32)],
        compiler_params=_params(("parallel", "parallel")),
        name="dn_prep",
    )(dqkv, dqkv, dqkv, conv_w, braw, araw, alog, dtb, bd, trif, trib, tot, sel, exf, exb)


def _dn_scan_kernel(kn_ref, qn_ref, vb_ref, win_ref, qg_ref, kdt_ref, gc_ref, gct_ref, beta_ref, gl_ref,
                    o_ref, state, *, rev, cpb, gl_rows):
    j = pl.program_id(1)
    nb = pl.num_programs(1)

    @pl.when(j == 0)
    def _():
        state[...] = jnp.zeros(state.shape, F32)

    jj = (nb - 1 - j) if rev else j
    ri = lax.broadcasted_iota(jnp.int32, (CHUNK, CHUNK), 0)
    ci = lax.broadcasted_iota(jnp.int32, (CHUNK, CHUNK), 1)
    incl = (ri <= ci) if rev else (ri >= ci)
    strict = (ri < ci) if rev else (ri > ci)
    eye = jnp.where(ri == ci, 1.0, 0.0)
    merge = [((ri // (2 * sz)) == (ci // (2 * sz))) & ((ri // sz) != (ci // sz))
             for sz in (1, 2, 4, 8, 16, 32)]

    order = list(range(cpb - 1, -1, -1) if rev else range(cpb))
    heads = range(DN_HEADS)
    items = [(c, h) for c in order for h in heads]
    rows = {c: slice(c * CHUNK, (c + 1) * CHUNK) for c in order}
    ls = {h: slice(h * DN_HEAD_DIM, (h + 1) * DN_HEAD_DIM) for h in heads}
    k_c = {c: kn_ref[0, rows[c], :] for c in order}
    q_c = {c: qn_ref[0, rows[c], :] for c in order}
    gc_c = {c: gc_ref[0, 0, rows[c], :] for c in order}
    gct_c = {c: gct_ref[0, 0, :, rows[c]] for c in order}
    beta_c = {c: beta_ref[0, 0, rows[c], :] for c in order}
    a_kk = {(c, h): lax.dot_general(k_c[c][:, ls[h]], k_c[c][:, ls[h]], NT_DIMS, preferred_element_type=F32)
            for c, h in items}
    a_qk = {(c, h): lax.dot_general(q_c[c][:, ls[h]], k_c[c][:, ls[h]], NT_DIMS, preferred_element_type=F32)
            for c, h in items}
    dec, lm, t = {}, {}, {}
    for c, h in items:
        diff = gc_c[c][:, h:h + 1] - gct_c[c][h:h + 1, :]
        dec[c, h] = jnp.exp(jnp.where(incl, diff, -jnp.inf))
        lm[c, h] = jnp.where(strict, a_kk[c, h] * beta_c[c][:, h:h + 1] * dec[c, h], 0.0)
        t[c, h] = eye - jnp.where(merge[0], lm[c, h], 0.0)
    for mk in merge[1:]:
        tb = {i: t[i].astype(BF16) for i in items}
        x = {i: _dot(jnp.where(mk, lm[i], 0.0).astype(BF16), tb[i]).astype(BF16) for i in items}
        t = {i: t[i] - _dot(tb[i], x[i]) for i in items}
    tb = {i: t[i].astype(BF16) for i in items}
    u = {(c, h): _dot(tb[c, h], vb_ref[0, 0, rows[c], ls[h]]) for c, h in items}
    w = {(c, h): _dot(tb[c, h], win_ref[0, 0, rows[c], ls[h]]).astype(BF16) for c, h in items}
    a_in = {i: (a_qk[i] * dec[i]).astype(BF16) for i in items}

    for c in order:
        gl_c = gl_ref[0, 0, pl.ds((jj * cpb) % gl_rows + c, 1), :]
        s_old = {h: state[h] for h in heads}
        sb = {h: s_old[h].astype(BF16) for h in heads}
        ws = {h: _dot(w[c, h], sb[h]) for h in heads}
        qs = {h: _dot(qg_ref[0, 0, rows[c], ls[h]], sb[h]) for h in heads}
        vnb = {h: (u[c, h] - ws[h]).astype(BF16) for h in heads}
        o_in = {h: _dot(a_in[c, h], vnb[h]) for h in heads}
        ds = {h: _dot(kdt_ref[0, 0, ls[h], rows[c]], vnb[h]) for h in heads}
        for h in heads:
            state[h] = s_old[h] * gl_c[:, ls[h]] + ds[h]
        o_ref[0, rows[c], :] = jnp.concatenate([qs[h] + o_in[h] for h in heads], axis=1)


def _dn_scan(prep, rev, tsb, gl_rows):
    kn, qn, vb, win, qg, kdt, gc, gct, beta, gl = prep
    bsz, s, _ = kn.shape
    nb = s // tsb
    cpb = tsb // CHUNK
    d = 1 if rev else 0
    blk = (lambda j: nb - 1 - j) if rev else (lambda j: j)
    tok = pl.BlockSpec((1, tsb, DN_WIDTH), lambda b, j: (b, blk(j), 0))
    dtok = lambda w: pl.BlockSpec((1, 1, tsb, w), lambda b, j: (d, b, blk(j), 0))
    return pl.pallas_call(
        functools.partial(_dn_scan_kernel, rev=rev, cpb=cpb, gl_rows=gl_rows),
        grid=(bsz, nb),
        in_specs=[tok, tok, dtok(DN_WIDTH), dtok(DN_WIDTH), dtok(DN_WIDTH),
                  pl.BlockSpec((1, 1, DN_WIDTH, tsb), lambda b, j: (d, b, 0, blk(j))),
                  dtok(LANES),
                  pl.BlockSpec((1, 1, DN_HEADS, tsb), lambda b, j: (d, b, 0, blk(j))),
                  dtok(LANES),
                  pl.BlockSpec((1, 1, gl_rows, DN_WIDTH), lambda b, j: (d, b, (blk(j) * cpb) // gl_rows, 0))],
        out_specs=tok,
        out_shape=jax.ShapeDtypeStruct((bsz, s, DN_WIDTH), F32),
        scratch_shapes=[pltpu.VMEM((DN_HEADS, DN_HEAD_DIM, DN_HEAD_DIM), F32)],
        compiler_params=_params(("parallel", "arbitrary")),
        name="dn_scan_bwd" if rev else "dn_scan_fwd",
    )(kn, qn, vb, win, qg, kdt, gc, gct, beta, gl)


def _post_kernel(attn_ref, of_ref, ob_ref, dz_ref, g_ref, x_ref, gt1_ref, sh2_ref, sc2_ref,
                 dnw_ref, n2w_ref, bd_ref, wau_ref, wdu_ref, wo_ref, wrh_ref, wrl_ref,
                 x1_ref, h2_ref, afft_ref, *, parts):
    tm, d = x_ref.shape[1], x_ref.shape[2]
    rows = [slice(i * (tm // parts), (i + 1) * (tm // parts)) for i in range(parts)]
    o = [of_ref[0, r, :] + ob_ref[0, r, :] for r in rows]
    ms = [_split_dot(v * v, bd_ref[...], 2) for v in o]
    dn = [(v * lax.rsqrt(m + EPS) * dnw_ref[...] * _silu(dz_ref[0, r, :].astype(F32))).astype(BF16)
          for v, m, r in zip(o, ms, rows)]
    up_a = [_dot(attn_ref[0, r, :], wau_ref[...]) for r in rows]
    up_d = [_dot(v, wdu_ref[...]) for v in dn]
    merged = [(jax.nn.sigmoid(g_ref[0, r, :d].astype(F32)) * a
               + jax.nn.sigmoid(g_ref[0, r, d:].astype(F32)) * b).astype(BF16)
              for r, a, b in zip(rows, up_a, up_d)]
    mixed = [_dot(v, wo_ref[...]) for v in merged]
    h2s = []
    for r, v in zip(rows, mixed):
        x1 = x_ref[0, r, :] + gt1_ref[0] * v
        x1_ref[0, r, :] = x1
        ms2 = jnp.mean(x1 * x1, axis=-1, keepdims=True)
        h2 = x1 * lax.rsqrt(ms2 + EPS) * n2w_ref[...]
        h2 = h2 * (1.0 + sc2_ref[0]) + sh2_ref[0]
        h2_ref[0, r, :] = h2.astype(BF16)
        h2s.append(h2)
    his = [v.astype(BF16) for v in h2s]
    los = [(v - h.astype(F32)).astype(BF16) for v, h in zip(h2s, his)]
    logits = [_dot(h, wrh_ref[...]) + _dot(l, wrh_ref[...]) + _dot(h, wrl_ref[...]) for h, l in zip(his, los)]
    for r, lg in zip(rows, logits):
        lane = lax.broadcasted_iota(jnp.int32, lg.shape, 1)
        lg = jnp.where(lane < N_EXPERTS, lg, -jnp.inf)
        e = jnp.exp(lg - jnp.max(lg, axis=1, keepdims=True))
        aff = e / jnp.sum(e, axis=1, keepdims=True)
        afft_ref[0, :, r] = aff.T[:N_EXPERTS]


def _post(attn, o_f, o_b, dz, graw, x, gt1, sh2, sc2, dnw, n2w, wau, wdu, wo, wr, tm):
    bsz, s, d = x.shape
    bd = _block_diag(DN_WIDTH, DN_HEAD_DIM, 1.0 / DN_HEAD_DIM)
    dnw_t = jnp.tile(dnw, DN_HEADS).reshape(1, DN_WIDTH)
    wr_p = jnp.zeros((d, LANES), F32).at[:, :N_EXPERTS].set(wr)
    wr_hi = wr_p.astype(BF16)
    wr_lo = (wr_p - wr_hi.astype(F32)).astype(BF16)
    tok = lambda w: pl.BlockSpec((1, tm, w), lambda b, i: (b, i, 0))
    full = lambda t: pl.BlockSpec(t.shape, lambda b, i: (0,) * t.ndim)
    vec = pl.BlockSpec((1, 1, d), lambda b, i: (b, 0, 0))
    sds = jax.ShapeDtypeStruct
    return pl.pallas_call(
        functools.partial(_post_kernel, parts=2 if tm % 256 == 0 else 1),
        grid=(bsz, s // tm),
        in_specs=[tok(ATTN_WIDTH), tok(DN_WIDTH), tok(DN_WIDTH), tok(DN_WIDTH), tok(2 * d), tok(d),
                  vec, vec, vec, full(dnw_t), full(n2w), full(bd), full(wau), full(wdu), full(wo),
                  full(wr_hi), full(wr_lo)],
        out_specs=[tok(d), tok(d), pl.BlockSpec((1, N_EXPERTS, tm), lambda b, i: (b, 0, i))],
        out_shape=[sds((bsz, s, d), F32), sds((bsz, s, d), BF16), sds((bsz, N_EXPERTS, s), F32)],
        compiler_params=_params(("parallel", "parallel")),
        name="post_mixer",
    )(attn, o_f, o_b, dz, graw, x, gt1, sh2, sc2, dnw_t, n2w, bd, wau, wdu, wo, wr_hi, wr_lo)


def _topk_kernel(aff_ref, tri_ref, code_ref, *, cap, s):
    a = aff_ref[0]
    ne = a.shape[0]
    count = lambda m: jnp.sum(jnp.where(m, 1.0, 0.0), axis=1, keepdims=True)

    def vbody(i, lo):
        cand = lo | (jnp.int32(1) << (30 - i))
        return jnp.where(count(a >= pltpu.bitcast(cand, F32)) >= cap, cand, lo)

    lo_bits = lax.fori_loop(0, 31, vbody, jnp.zeros((ne, 1), jnp.int32))
    lo = pltpu.bitcast(lo_bits, F32)
    hi = pltpu.bitcast(lo_bits + 1, F32)

    def rbody(i, lh):
        lo, hi = lh
        mid = 0.5 * (lo + hi)
        ok = count(a >= mid) >= cap
        return jnp.where(ok, mid, lo), jnp.where(ok, hi, mid)

    lo, hi = lax.fori_loop(0, 32, rbody, (lo, hi))
    gt = a >= hi
    eq = (a >= lo) & jnp.logical_not(gt)
    need = cap - count(gt)
    idx = lax.broadcasted_iota(jnp.int32, a.shape, 1)
    nbits = int(np.log2(s))

    def ibody(i, x):
        cand = x | (jnp.int32(1) << (nbits - 1 - i))
        return jnp.where(count(eq & (idx < cand)) < need, cand, x)

    last = lax.fori_loop(0, nbits, ibody, jnp.zeros((ne, 1), jnp.int32))
    sel = gt | (eq & (idx <= last))
    self32 = jnp.where(sel, 1.0, 0.0)
    tri = tri_ref[...]
    carry = jnp.zeros((ne, 1), F32)
    for t in range(s // LANES):
        seg = self32[:, t * LANES:(t + 1) * LANES]
        inc = _dot(seg.astype(BF16), tri)
        pos = (inc - seg + carry).astype(jnp.int32)
        code_ref[0, :, t * LANES:(t + 1) * LANES] = pos * 2 + seg.astype(jnp.int32)
        carry = carry + inc[:, LANES - 1:LANES]


def _topk(afft, cap):
    bsz, ne, s = afft.shape
    i = np.arange(LANES)
    tri = jnp.asarray(np.where(i[:, None] <= i[None, :], 1.0, 0.0), BF16)
    return pl.pallas_call(
        functools.partial(_topk_kernel, cap=cap, s=s),
        grid=(bsz,),
        in_specs=[pl.BlockSpec((1, ne, s), lambda b: (b, 0, 0)),
                  pl.BlockSpec((LANES, LANES), lambda b: (0, 0))],
        out_specs=pl.BlockSpec((1, ne, s), lambda b: (b, 0, 0)),
        out_shape=jax.ShapeDtypeStruct((bsz, ne, s), jnp.int32),
        compiler_params=_params(("parallel",)),
        name="topk",
    )(afft, tri)


def _moe_ffn_kernel(starts_ref, code_ref, aff_ref, h2_ref, wg_ref, wu_ref, wd_ref, y_ref, xe, gacc,
                    *, tb, win, cap, nj, nsub, spt):
    b, e, j = pl.program_id(0), pl.program_id(1), pl.program_id(2)

    @pl.when(j == 0)
    def _():
        xe[...] = jnp.zeros(xe.shape, F32)
        gacc[...] = jnp.zeros(gacc.shape, F32)

    code = code_ref[0, 0, 0]
    aff = aff_ref[0, 0, 0]
    base = ((b * pl.num_programs(1) + e) * nj + j) * nsub
    a = [pl.multiple_of(starts_ref[(base + k) * spt] // 16 * 16, 16) for k in range(nsub)]
    hot = []
    for k in range(nsub):
        r = lax.broadcasted_iota(jnp.int32, (win, tb), 0) + a[k]
        hot.append(code[:, k * tb:(k + 1) * tb] == 2 * r + 1)
    rows = [_dot(jnp.where(hot[k], 1.0, 0.0).astype(BF16), h2_ref[0, k * tb:(k + 1) * tb, :]) for k in range(nsub)]
    gates = [jnp.sum(jnp.where(hot[k], aff[:, k * tb:(k + 1) * tb], 0.0), axis=1, keepdims=True)
             for k in range(nsub)]
    for k in range(nsub):
        xe[pl.ds(a[k], win), :] = xe[pl.ds(a[k], win), :] + rows[k]
        gacc[pl.ds(a[k], win), :] = gacc[pl.ds(a[k], win), :] + gates[k]

    @pl.when(j == nj - 1)
    def _():
        xb = xe[0:cap, :].astype(BF16)
        hg = _dot(xb, wg_ref[0, 0].astype(BF16))
        hu = _dot(xb, wu_ref[0, 0].astype(BF16))
        act = (_silu(hg) * hu).astype(BF16)
        y_ref[0, 0, 0:cap, :] = (_dot(act, wd_ref[0, 0].astype(BF16)) * gacc[0:cap, :]).astype(BF16)
        y_ref[0, 0, cap:, :] = jnp.zeros((y_ref.shape[2] - cap, y_ref.shape[3]), BF16)


def _moe_ffn(starts, code, afft, h2, w_gate, w_up, w_down, cap, tb, win, capp, spt):
    bsz, s, d = h2.shape
    ne = code.shape[1]
    nsub = min(8, s // tb)
    tstep = nsub * tb
    nj = s // tstep
    f = w_gate.shape[-1]
    row = pl.BlockSpec((1, 1, 1, 1, tstep), lambda b, e, j, st: (b, e, j, 0, 0))
    wspec = lambda r, c: pl.BlockSpec((1, 1, r, c), lambda b, e, j, st: (0, e, 0, 0))
    return pl.pallas_call(
        functools.partial(_moe_ffn_kernel, tb=tb, win=win, cap=cap, nj=nj, nsub=nsub, spt=spt),
        grid_spec=pltpu.PrefetchScalarGridSpec(
            num_scalar_prefetch=1,
            grid=(bsz, ne, nj),
            in_specs=[row, row,
                      pl.BlockSpec((1, tstep, d), lambda b, e, j, st: (b, j, 0)),
                      wspec(d, f), wspec(d, f), wspec(f, d)],
            out_specs=pl.BlockSpec((1, 1, capp, d), lambda b, e, j, st: (b, e, 0, 0)),
            scratch_shapes=[pltpu.VMEM((capp, d), F32), pltpu.VMEM((capp, 1), F32)]),
        out_shape=jax.ShapeDtypeStruct((bsz, ne, capp, d), BF16),
        compiler_params=_params(("parallel", "parallel", "arbitrary")),
        name="moe_ffn",
    )(starts, code.reshape(bsz, ne, nj, 1, tstep), afft.reshape(bsz, ne, nj, 1, tstep), h2, w_gate, w_up, w_down)


TN_DIMS = (((0,), (0,)), ((), ()))


def _moe_scatter_kernel(starts_ref, code_ref, *refs, ne, tb, sub, swin, ns):
    y_refs = refs[:ne]
    x1_ref, gt2_ref, o_ref = refs[ne:]
    b, j = pl.program_id(0), pl.program_id(1)
    spb = tb // sub
    for k in range(spb):
        acc = None
        for e in range(ne):
            first = (b * ne + e) * ns + j * spb
            a_blk = starts_ref[first] // 16 * 16
            a_sub = starts_ref[first + k] // 16 * 16
            off = pl.multiple_of(a_sub - a_blk, 16)
            r = lax.broadcasted_iota(jnp.int32, (swin, sub), 0) + a_sub
            hot = code_ref[0, e:e + 1, k * sub:(k + 1) * sub] == 2 * r + 1
            z = lax.dot_general(jnp.where(hot, 1.0, 0.0).astype(BF16), y_refs[e][pl.ds(off, swin), :], TN_DIMS,
                                preferred_element_type=F32)
            acc = z if acc is None else acc + z
        rows = slice(k * sub, (k + 1) * sub)
        o_ref[0, rows, :] = x1_ref[0, rows, :] + gt2_ref[0] * acc


def _moe_scatter(starts, code, y, x1, gt2, tb, win, sub):
    bsz, s, d = x1.shape
    ne = code.shape[1]
    nj = s // tb
    ns = s // sub
    spb = tb // sub

    def window(e):
        return pl.BlockSpec((pl.squeezed, pl.squeezed, pl.Element(win), pl.Element(d)),
                            lambda b, j, st: (b, e, pl.multiple_of(st[(b * ne + e) * ns + j * spb] // 16 * 16, 16), 0))

    return pl.pallas_call(
        functools.partial(_moe_scatter_kernel, ne=ne, tb=tb, sub=sub, swin=sub + 16, ns=ns),
        grid_spec=pltpu.PrefetchScalarGridSpec(
            num_scalar_prefetch=1,
            grid=(bsz, nj),
            in_specs=[pl.BlockSpec((1, ne, tb), lambda b, j, st: (b, 0, j))]
                     + [window(e) for e in range(ne)]
                     + [pl.BlockSpec((1, tb, d), lambda b, j, st: (b, j, 0)),
                        pl.BlockSpec((1, 1, d), lambda b, j, st: (b, 0, 0))],
            out_specs=pl.BlockSpec((1, tb, d), lambda b, j, st: (b, j, 0))),
        out_shape=jax.ShapeDtypeStruct((bsz, s, d), F32),
        compiler_params=_params(("parallel", "parallel")),
        name="moe_scatter",
    )(starts, code, *([y] * ne), x1, gt2)


def _tile(s, pref):
    t = pref
    while s % t:
        t //= 2
    return t


def _layer(x, c, w_ada, b_ada, norm1_w, w_in, q_norm_w, k_norm_w, conv_w, a_log, dt_bias, dn_norm_w,
           w_attn_up, w_dn_up, w_o, norm2_w, w_router, w_gate, w_up, w_down):
    bsz, s, d = x.shape
    mod = _ada(c, w_ada, b_ada)
    sh1, sc1, gt1, sh2, sc2, gt2 = [m.reshape(bsz, 1, d) for m in jnp.split(mod, 6, axis=-1)]

    o0 = ATTN_WIDTH + 2 * KV_WIDTH
    o1 = o0 + 3 * DN_WIDTH
    o2 = o1 + DN_WIDTH
    o3 = o2 + 2 * DN_HEADS
    o4 = o3 + 2 * DN_HEADS
    pad = lambda w: jnp.zeros((d, LANES), w.dtype).at[:, :w.shape[1]].set(w)
    ws = [w_in[:, :o0], w_in[:, o0:o1], w_in[:, o1:o2], pad(w_in[:, o2:o3]), pad(w_in[:, o3:o4]), w_in[:, o4:]]
    ws = [w.astype(BF16) for w in ws]
    a_qkv, dqkv, dz, braw, araw, graw = _inproj(x, norm1_w.reshape(1, d), sh1, sc1, ws, _tile(s, 512))

    q, kt, v = _attn_prep(a_qkv, q_norm_w, k_norm_w, _tile(s, 512))
    attn = _attention(q, kt, v, _tile(s, 256), _tile(s, 1024))

    ts = 512
    prep = _dn_prep(dqkv, braw, araw, conv_w, a_log, dt_bias, ts)
    o_f = _dn_scan(prep, False, 2 * CHUNK, ts // CHUNK)
    o_b = _dn_scan(prep, True, 2 * CHUNK, ts // CHUNK)

    x1, h2, afft = _post(attn, o_f, o_b, dz, graw, x, gt1, sh2, sc2, dn_norm_w, norm2_w.reshape(1, d),
                               w_attn_up.astype(BF16), w_dn_up.astype(BF16), w_o.astype(BF16), w_router,
                               _tile(s, 512))

    cap = CAPACITY_FACTOR * s // N_EXPERTS
    tb, sub = 256, 128
    win = tb + 16
    code = _topk(afft, cap)
    starts = (code[:, :, ::sub] >> 1).reshape(-1)
    y = _moe_ffn(starts, code, afft, h2, w_gate[None], w_up[None], w_down[None], cap, tb, win, cap + win, tb // sub)
    return _moe_scatter(starts, code, y, x1, gt2, tb, win, sub)


def kernel(x, c, w_ada, b_ada, norm1_w, w_in, q_norm_w, k_norm_w, conv_w, a_log, dt_bias, dn_norm_w,
           w_attn_up, w_dn_up, w_o, norm2_w, w_router, w_gate, w_up, w_down):
    depth = w_ada.shape[0]
    for l in range(depth):
        x = _layer(x, c, w_ada[l], b_ada[l], norm1_w[l], w_in[l], q_norm_w[l], k_norm_w[l], conv_w[l],
                   a_log[l], dt_bias[l], dn_norm_w[l], w_attn_up[l], w_dn_up[l], w_o[l], norm2_w[l],
                   w_router[l], w_gate[l], w_up[l], w_down[l])
    return x
```

```python
import functools

import numpy as np
import jax
import jax.numpy as jnp
from jax import lax
from jax.experimental import pallas as pl
from jax.experimental.pallas import tpu as pltpu

F32 = jnp.float32
BF16 = jnp.bfloat16

GRID_W = 64
N_Q_HEADS = 8
N_KV_HEADS = 2
HEAD_DIM = 64
ATTN_WIDTH = N_Q_HEADS * HEAD_DIM
KV_WIDTH = N_KV_HEADS * HEAD_DIM
ROPE_THETA = 10000.0
DN_HEADS = 8
DN_HEAD_DIM = 64
DN_WIDTH = DN_HEADS * DN_HEAD_DIM
CONV_WIDTH = 5
CHUNK = 64
N_EXPERTS = 16
CAPACITY_FACTOR = 2
EPS = 1e-6
LOG2E = 1.4426950408889634
LANES = 128
VMEM_LIMIT = 56 * 1024 * 1024

NT_DIMS = (((1,), (1,)), ((), ()))


def _params(sem):
    return pltpu.CompilerParams(dimension_semantics=sem, vmem_limit_bytes=VMEM_LIMIT)


def _dot(a, b):
    return jnp.dot(a, b, preferred_element_type=F32)


def _split_dot(x, m, parts):
    acc = None
    r = x
    for i in range(parts):
        h = r.astype(BF16)
        d = _dot(h, m)
        acc = d if acc is None else acc + d
        if i + 1 < parts:
            r = r - h.astype(F32)
    return acc


def _split_dot_l(m, x, parts):
    acc = None
    r = x
    for i in range(parts):
        h = r.astype(BF16)
        d = _dot(m, h)
        acc = d if acc is None else acc + d
        if i + 1 < parts:
            r = r - h.astype(F32)
    return acc


def _silu(x):
    return x * jax.nn.sigmoid(x)


def _ada_kernel(c_ref, w_ref, b_ref, o_ref):
    c = c_ref[...]
    o_ref[...] = jnp.dot(_silu(c), w_ref[...], preferred_element_type=F32,
                         precision=lax.Precision.HIGHEST) + b_ref[...]


def _ada(c, w, b):
    bsz, d = c.shape
    n = w.shape[1]
    tn = 1536
    rows = 8
    c8 = jnp.zeros((rows, d), F32).at[:bsz].set(c)
    out = pl.pallas_call(
        _ada_kernel,
        grid=(n // tn,),
        in_specs=[pl.BlockSpec((rows, d), lambda j: (0, 0)),
                  pl.BlockSpec((d, tn), lambda j: (0, j)),
                  pl.BlockSpec((1, tn), lambda j: (0, j))],
        out_specs=pl.BlockSpec((rows, tn), lambda j: (0, j)),
        out_shape=jax.ShapeDtypeStruct((rows, n), F32),
        compiler_params=_params(("arbitrary",)),
        name="ada",
    )(c8, w, b.reshape(1, n))
    return out[:bsz]


def _inproj_kernel(x_ref, nw_ref, sh_ref, sc_ref, wa_ref, wd_ref, wz_ref, wb_ref, wal_ref, wg_ref,
                   oa_ref, od_ref, oz_ref, ob_ref, oal_ref, og_ref):
    x = x_ref[0]
    ms = jnp.mean(x * x, axis=-1, keepdims=True)
    h = x * lax.rsqrt(ms + EPS) * nw_ref[...]
    h = h * (1.0 + sc_ref[0]) + sh_ref[0]
    hb = h.astype(BF16)
    oa_ref[0] = _dot(hb, wa_ref[...]).astype(BF16)
    od_ref[0] = _dot(hb, wd_ref[...]).astype(BF16)
    oz_ref[0] = _dot(hb, wz_ref[...]).astype(BF16)
    ob_ref[0] = _dot(hb, wb_ref[...])
    oal_ref[0] = _dot(hb, wal_ref[...])
    og_ref[0] = _dot(hb, wg_ref[...]).astype(BF16)


def _inproj(x, nw, sh, sc, ws, tm):
    bsz, s, d = x.shape
    widths = [w.shape[1] for w in ws]
    dts = [BF16, BF16, BF16, F32, F32, BF16]
    tok = lambda w: pl.BlockSpec((1, tm, w), lambda b, i: (b, i, 0))
    full = lambda a: pl.BlockSpec(a.shape, lambda b, i: (0,) * a.ndim)
    vec = pl.BlockSpec((1, 1, d), lambda b, i: (b, 0, 0))
    return pl.pallas_call(
        _inproj_kernel,
        grid=(bsz, s // tm),
        in_specs=[tok(d), full(nw), vec, vec] + [full(w) for w in ws],
        out_specs=[tok(w) for w in widths],
        out_shape=[jax.ShapeDtypeStruct((bsz, s, w), dt) for w, dt in zip(widths, dts)],
        compiler_params=_params(("parallel", "parallel")),
        name="inproj",
    )(x, nw, sh, sc, *ws)


def _norm_rope(x, w, bd, cos, sa, sb):
    n = x.shape[1]
    ms = _dot((x * x).astype(BF16), bd)
    xn = x * lax.rsqrt(ms + EPS) * w
    return xn * cos + pltpu.roll(xn, n - 16, 1) * sa + pltpu.roll(xn, 16, 1) * sb


def _attn_prep_kernel(a_ref, cos_ref, sa_ref, sb_ref, qw_ref, kw_ref, bdq_ref, bdk_ref,
                      q_ref, kt_ref, v_ref):
    a = a_ref[0].astype(F32)
    q = a[:, :ATTN_WIDTH]
    k = a[:, ATTN_WIDTH:ATTN_WIDTH + KV_WIDTH]
    v = a[:, ATTN_WIDTH + KV_WIDTH:]
    cos, sa, sb = cos_ref[...], sa_ref[...], sb_ref[...]
    rep = ATTN_WIDTH // LANES
    tile = lambda t: jnp.concatenate([t] * rep, axis=1)
    qr = _norm_rope(q, qw_ref[...], bdq_ref[...], tile(cos), tile(sa), tile(sb))
    q_ref[0] = (qr * (HEAD_DIM ** -0.5 * LOG2E)).astype(BF16)
    kr = _norm_rope(k, kw_ref[...], bdk_ref[...], cos, sa, sb)
    kt = kr.T
    kt_ref[0, 0] = kt[:HEAD_DIM].astype(BF16)
    kt_ref[0, 1] = kt[HEAD_DIM:].astype(BF16)
    lane = lax.broadcasted_iota(jnp.int32, v.shape, 1)
    ones_col = jnp.where(lane == HEAD_DIM, 1.0, 0.0)
    v_ref[0, 0] = jnp.where(lane < HEAD_DIM, v, ones_col).astype(BF16)
    v_ref[0, 1] = jnp.where(lane < HEAD_DIM, pltpu.roll(v, HEAD_DIM, 1), ones_col).astype(BF16)


def _rope_tables(s):
    pos = jnp.arange(s)
    lane = jnp.arange(LANES)
    d = lane % HEAD_DIM
    p = d % 32
    f = (p % 16).astype(F32)
    freqs = ROPE_THETA ** (-(f * 2.0 / 32.0))
    axis_pos = jnp.where((d // 32)[None, :] == 0, (pos // GRID_W)[:, None], (pos % GRID_W)[:, None])
    ang = axis_pos.astype(F32) * freqs[None, :]
    cos, sin = jnp.cos(ang), jnp.sin(ang)
    first = (p < 16)[None, :]
    return cos, jnp.where(first, -sin, 0.0), jnp.where(first, 0.0, sin)


def _block_diag(n, blk, val):
    i = np.arange(n)
    return jnp.asarray(np.where((i[:, None] // blk) == (i[None, :] // blk), val, 0.0), BF16)


def _attn_prep(a, qw, kw, tp):
    bsz, s, wa = a.shape
    cos, sa, sb = _rope_tables(s)
    bdq = _block_diag(ATTN_WIDTH, HEAD_DIM, 1.0 / HEAD_DIM)
    bdk = _block_diag(KV_WIDTH, HEAD_DIM, 1.0 / HEAD_DIM)
    qw_t = jnp.tile(qw, N_Q_HEADS).reshape(1, ATTN_WIDTH)
    kw_t = jnp.tile(kw, N_KV_HEADS).reshape(1, KV_WIDTH)
    tab = pl.BlockSpec((tp, LANES), lambda b, i: (i, 0))
    full = lambda t: pl.BlockSpec(t.shape, lambda b, i: (0,) * t.ndim)
    return pl.pallas_call(
        _attn_prep_kernel,
        grid=(bsz, s // tp),
        in_specs=[pl.BlockSpec((1, tp, wa), lambda b, i: (b, i, 0)), tab, tab, tab,
                  full(qw_t), full(kw_t), full(bdq), full(bdk)],
        out_specs=[pl.BlockSpec((1, tp, ATTN_WIDTH), lambda b, i: (b, i, 0)),
                   pl.BlockSpec((1, N_KV_HEADS, HEAD_DIM, tp), lambda b, i: (b, 0, 0, i)),
                   pl.BlockSpec((1, N_KV_HEADS, tp, LANES), lambda b, i: (b, 0, i, 0))],
        out_shape=[jax.ShapeDtypeStruct((bsz, s, ATTN_WIDTH), BF16),
                   jax.ShapeDtypeStruct((bsz, N_KV_HEADS, HEAD_DIM, s), BF16),
                   jax.ShapeDtypeStruct((bsz, N_KV_HEADS, s, LANES), BF16)],
        compiler_params=_params(("parallel", "parallel")),
        name="attn_prep",
    )(a, cos, sa, sb, qw_t, kw_t, bdq, bdk)


def _attn_kernel(q_ref, kt_ref, v_ref, o_ref, *, tq, tk, group):
    rows = group * tq
    q = q_ref[0]
    qs = jnp.concatenate([q[:, h * HEAD_DIM:(h + 1) * HEAD_DIM] for h in range(group)], axis=0)
    nk = kt_ref.shape[3] // tk
    scores = lambda k: _dot(qs, kt_ref[0, 0, :, k * tk:(k + 1) * tk])
    m = jnp.full((rows, LANES), -jnp.inf, F32)
    acc = jnp.zeros((rows, LANES), F32)
    s_next = scores(0)
    for k in range(nk):
        s = s_next
        if k + 1 < nk:
            s_next = scores(k + 1)
        m_new = jnp.maximum(m, jnp.max(s, axis=1, keepdims=True))
        alpha = jnp.exp2(m - m_new)
        p = jnp.concatenate([jnp.exp2(s[:, t * LANES:(t + 1) * LANES] - m_new).astype(BF16)
                             for t in range(tk // LANES)], axis=1)
        acc = alpha * acc + _dot(p, v_ref[0, 0, k * tk:(k + 1) * tk, :])
        m = m_new
    o = acc[:, :HEAD_DIM] * (1.0 / acc[:, HEAD_DIM:HEAD_DIM + 1])
    o_ref[0] = jnp.concatenate([o[h * tq:(h + 1) * tq] for h in range(group)], axis=1).astype(BF16)


def _attention(q, kt, v, tq, tk):
    bsz, s, _ = q.shape
    group = N_Q_HEADS // N_KV_HEADS
    gw = group * HEAD_DIM
    return pl.pallas_call(
        functools.partial(_attn_kernel, tq=tq, tk=tk, group=group),
        grid=(bsz, N_KV_HEADS, s // tq),
        in_specs=[pl.BlockSpec((1, tq, gw), lambda b, g, i: (b, i, g)),
                  pl.BlockSpec((1, 1, HEAD_DIM, s), lambda b, g, i: (b, g, 0, 0)),
                  pl.BlockSpec((1, 1, s, LANES), lambda b, g, i: (b, g, 0, 0))],
        out_specs=pl.BlockSpec((1, tq, gw), lambda b, g, i: (b, i, g)),
        out_shape=jax.ShapeDtypeStruct((bsz, s, ATTN_WIDTH), BF16),
        compiler_params=_params(("parallel", "parallel", "parallel")),
        name="attention",
    )(q, kt, v)


def _dn_prep_kernel(cur_ref, prev_ref, next_ref, cw_ref, braw_ref, araw_ref, alog_ref, dtb_ref,
                    bd_ref, trif_ref, trib_ref, sel_ref, exf_ref, exb_ref,
                    kn_ref, qn_ref, vb_ref, win_ref, qg_ref, kdt_ref, gcx_ref, betax_ref, gl_ref,
                    *, ts):
    i = pl.program_id(1)
    halo = prev_ref.shape[1]
    ext = jnp.concatenate([jnp.where(i > 0, prev_ref[0].astype(F32), 0.0),
                           cur_ref[0].astype(F32),
                           jnp.where(i < pl.num_programs(1) - 1, next_ref[0].astype(F32), 0.0)], axis=0)
    cw = cw_ref[...]
    conv = None
    for j in range(CONV_WIDTH):
        shift = (CONV_WIDTH // 2 - j) % ext.shape[0]
        tap = (pltpu.roll(ext, shift, 0) if shift else ext)[halo:halo + ts] * cw[j:j + 1, :]
        conv = tap if conv is None else conv + tap
    act = _silu(conv)
    cq, ck, cv = act[:, :DN_WIDTH], act[:, DN_WIDTH:2 * DN_WIDTH], act[:, 2 * DN_WIDTH:]
    bd = bd_ref[...]
    qn = cq * lax.rsqrt(_dot((cq * cq).astype(BF16), bd) + EPS) * (DN_HEAD_DIM ** -0.5)
    kn = ck * lax.rsqrt(_dot((ck * ck).astype(BF16), bd) + EPS)
    kn_ref[0] = kn.astype(BF16)
    qn_ref[0] = qn.astype(BF16)

    beta = jax.nn.sigmoid(braw_ref[0])
    g = -jnp.exp(alog_ref[...]) * jax.nn.softplus(araw_ref[0] + dtb_ref[...])
    gl = jnp.exp(_split_dot_l(sel_ref[...], g, 3))
    gcs = [_split_dot_l(trif_ref[...], g, 3), _split_dot_l(trib_ref[...], g, 3)]
    for d, ex_ref in enumerate((exf_ref, exb_ref)):
        gc = gcs[d]
        ex = ex_ref[...]
        beta_x = _dot(beta.astype(BF16), ex)
        gc_x = _split_dot(gc, ex, 3)
        ek_x = _dot(jnp.exp(gcs[1 - d] - g).astype(BF16), ex)
        eg_x = jnp.exp(gc_x)
        vb_ref[d, 0] = (cv * beta_x).astype(BF16)
        win_ref[d, 0] = (kn * beta_x * eg_x).astype(BF16)
        qg_ref[d, 0] = (qn * eg_x).astype(BF16)
        kdt_ref[d, 0] = (kn * ek_x).T.astype(BF16)
        gcx_ref[d, 0] = gc_x
        betax_ref[d, 0] = beta_x
        gl_ref[d, 0] = _split_dot(gl, ex, 2)


def _dn_prep(dqkv, braw, araw, conv_w, a_log, dt_bias, ts):
    bsz, s, w3 = dqkv.shape
    nt = s // ts
    cpt = ts // CHUNK
    assert cpt == 8
    halo = 16
    hb = ts // halo
    idx = np.arange(ts)
    same = (idx[:, None] // CHUNK) == (idx[None, :] // CHUNK)
    trif = jnp.asarray(np.where(same & (idx[:, None] >= idx[None, :]), 1.0, 0.0), BF16)
    trib = jnp.asarray(np.where(same & (idx[:, None] <= idx[None, :]), 1.0, 0.0), BF16)
    sel = jnp.asarray(np.where(np.arange(cpt)[:, None] == (idx[None, :] // CHUNK), 1.0, 0.0), BF16)
    bd = _block_diag(DN_WIDTH, DN_HEAD_DIM, 1.0)
    lane = np.arange(DN_WIDTH) // DN_HEAD_DIM
    row = np.arange(LANES)
    exf = jnp.asarray(np.where(row[:, None] == lane[None, :], 1.0, 0.0), BF16)
    exb = jnp.asarray(np.where(row[:, None] == lane[None, :] + DN_HEADS, 1.0, 0.0), BF16)
    nh2 = 2 * DN_HEADS
    alog = jnp.zeros((1, LANES), F32).at[0, :nh2].set(a_log.reshape(nh2))
    dtb = jnp.zeros((1, LANES), F32).at[0, :nh2].set(dt_bias.reshape(nh2))
    full = lambda t: pl.BlockSpec(t.shape, lambda b, i: (0,) * t.ndim)
    tok = lambda w: pl.BlockSpec((1, ts, w), lambda b, i: (b, i, 0))
    dtok = lambda w: pl.BlockSpec((2, 1, ts, w), lambda b, i: (0, b, i, 0))
    sds = jax.ShapeDtypeStruct
    return pl.pallas_call(
        functools.partial(_dn_prep_kernel, ts=ts),
        grid=(bsz, nt),
        in_specs=[tok(w3),
                  pl.BlockSpec((1, halo, w3), lambda b, i: (b, jnp.maximum(i * hb - 1, 0), 0)),
                  pl.BlockSpec((1, halo, w3), lambda b, i: (b, jnp.minimum((i + 1) * hb, s // halo - 1), 0)),
                  full(conv_w), tok(LANES), tok(LANES), full(alog), full(dtb),
                  full(bd), full(trif), full(trib), full(sel), full(exf), full(exb)],
        out_specs=[tok(DN_WIDTH), tok(DN_WIDTH), dtok(DN_WIDTH), dtok(DN_WIDTH), dtok(DN_WIDTH),
                   pl.BlockSpec((2, 1, DN_WIDTH, ts), lambda b, i: (0, b, 0, i)),
                   dtok(DN_WIDTH), dtok(DN_WIDTH),
                   pl.BlockSpec((2, 1, cpt, DN_WIDTH), lambda b, i: (0, b, i, 0))],
        out_shape=[sds((bsz, s, DN_WIDTH), BF16), sds((bsz, s, DN_WIDTH), BF16),
                   sds((2, bsz, s, DN_WIDTH), BF16), sds((2, bsz, s, DN_WIDTH), BF16),
                   sds((2, bsz, s, DN_WIDTH), BF16), sds((2, bsz, DN_WIDTH, s), BF16),
                   sds((2, bsz, s, DN_WIDTH), F32), sds((2, bsz, s, DN_WIDTH), F32),
                   sds((2, bsz, s // CHUNK, DN_WIDTH), F32)],
        compiler_params=_params(("parallel", "parallel")),
        name="dn_prep",
    )(dqkv, dqkv, dqkv, conv_w, braw, araw, alog, dtb, bd, trif, trib, sel, exf, exb)


def _pair_diag(x):
    left = lax.broadcasted_iota(jnp.int32, x.shape, 1) < DN_HEAD_DIM
    zero = jnp.zeros_like(x)
    return jnp.concatenate([jnp.where(left, x, zero), jnp.where(left, zero, x)], axis=0)


def _dn_scan_kernel(*refs, cpb, gl_rows):
    nin = 9
    ins = (refs[:nin], refs[nin:2 * nin])
    outs = refs[2 * nin:2 * nin + 2]
    state = refs[2 * nin + 2]
    j = pl.program_id(1)
    nb = pl.num_programs(1)

    @pl.when(j == 0)
    def _():
        state[...] = jnp.zeros(state.shape, F32)

    ri = lax.broadcasted_iota(jnp.int32, (CHUNK, LANES), 0)
    ci = lax.broadcasted_iota(jnp.int32, (CHUNK, LANES), 1) % CHUNK
    incl = (ri >= ci, ri <= ci)
    strict = (ri > ci, ri < ci)
    diag = ri == ci
    eye = jnp.where(diag, 1.0, 0.0)
    merge = [((ri // (2 * sz)) == (ci // (2 * sz))) & ((ri // sz) != (ci // sz))
             for sz in (1, 2, 4, 8, 16, 32)]
    rd = lax.broadcasted_iota(jnp.int32, (LANES, LANES), 0) // DN_HEAD_DIM
    cd = lax.broadcasted_iota(jnp.int32, (LANES, LANES), 1) // DN_HEAD_DIM
    same_head = rd == cd

    dirs = (0, 1)
    pairs = range(DN_HEADS // 2)
    items = [(d, c, p) for d in dirs for c in range(cpb) for p in pairs]
    rows = [slice(c * CHUNK, (c + 1) * CHUNK) for c in range(cpb)]
    ls = [slice(p * LANES, (p + 1) * LANES) for p in pairs]
    kq = {(d, c, p): jnp.concatenate([ins[d][0][0, rows[c], ls[p]], ins[d][1][0, rows[c], ls[p]]], axis=0)
          for d, c, p in items}
    a = {i: lax.dot_general(kq[i], _pair_diag(kq[i][:CHUNK]), NT_DIMS, preferred_element_type=F32)
         for i in items}
    dec, lm, t = {}, {}, {}
    for d, c, p in items:
        gcx = ins[d][6][0, 0, rows[c], ls[p]]
        grow = jnp.sum(jnp.where(diag, gcx, 0.0), axis=0, keepdims=True)
        i = (d, c, p)
        dec[i] = jnp.exp(jnp.where(incl[d], gcx - grow, -jnp.inf))
        lm[i] = jnp.where(strict[d], a[i][:CHUNK] * ins[d][7][0, 0, rows[c], ls[p]] * dec[i], 0.0)
        t[i] = eye - jnp.where(merge[0], lm[i], 0.0)
    for mk in merge[1:]:
        x = {i: _dot(jnp.where(mk, lm[i], 0.0).astype(BF16), _pair_diag(t[i].astype(BF16))).astype(BF16)
             for i in items}
        t = {i: t[i] - _dot(t[i].astype(BF16), _pair_diag(x[i])) for i in items}
    uw = {(d, c, p): _dot(t[d, c, p].astype(BF16),
                          jnp.concatenate([_pair_diag(ins[d][2][0, 0, rows[c], ls[p]]),
                                           _pair_diag(ins[d][3][0, 0, rows[c], ls[p]])], axis=1))
          for d, c, p in items}
    a_in = {i: (a[i][CHUNK:] * dec[i]).astype(BF16) for i in items}

    zeros = jnp.zeros((CHUNK, LANES), BF16)
    blk = (j, nb - 1 - j)
    for step in range(cpb):
        chunk = (step, cpb - 1 - step)
        live = [(d, chunk[d], p) for d in dirs for p in pairs]
        gl_c = [ins[d][8][0, 0, pl.ds((blk[d] * cpb) % gl_rows + chunk[d], 1), :] for d in dirs]
        s_old = {(d, p): state[d, p] for d in dirs for p in pairs}
        wq = {(d, c, p): _dot(jnp.concatenate([uw[d, c, p][:, LANES:].astype(BF16),
                                               ins[d][4][0, 0, rows[c], ls[p]]], axis=0),
                              s_old[d, p].astype(BF16)) for d, c, p in live}
        vnb = {i: (uw[i][:, :LANES] - wq[i][:CHUNK]).astype(BF16) for i in live}
        o_in = {i: _dot(a_in[i], _pair_diag(vnb[i])) for i in live}
        ds = {(d, c, p): _dot(ins[d][5][0, 0, ls[p], (c // 2) * LANES:(c // 2 + 1) * LANES],
                              jnp.concatenate([vnb[d, c, p], zeros] if c % 2 == 0 else [zeros, vnb[d, c, p]], axis=0))
              for d, c, p in live}
        for d, c, p in live:
            state[d, p] = s_old[d, p] * gl_c[d][:, ls[p]] + jnp.where(same_head, ds[d, c, p], 0.0)
            outs[d][0, rows[c], ls[p]] = wq[d, c, p][CHUNK:] + o_in[d, c, p]


def _dn_scan(prep, tsb, gl_rows):
    kn, qn, vb, win, qg, kdt, gcx, betax, gl = prep
    bsz, s, _ = kn.shape
    nb = s // tsb
    cpb = tsb // CHUNK
    assert cpb % 2 == 0
    toks, ins = [], []
    for d in (0, 1):
        blk = (lambda j: j) if d == 0 else (lambda j: nb - 1 - j)
        tok = pl.BlockSpec((1, tsb, DN_WIDTH), lambda b, j, blk=blk: (b, blk(j), 0))
        dtok = pl.BlockSpec((1, 1, tsb, DN_WIDTH), lambda b, j, blk=blk, d=d: (d, b, blk(j), 0))
        toks.append(tok)
        ins += [tok, tok, dtok, dtok, dtok,
                pl.BlockSpec((1, 1, DN_WIDTH, tsb), lambda b, j, blk=blk, d=d: (d, b, 0, blk(j))),
                dtok, dtok,
                pl.BlockSpec((1, 1, gl_rows, DN_WIDTH), lambda b, j, blk=blk, d=d: (d, b, (blk(j) * cpb) // gl_rows, 0))]
    args = (kn, qn, vb, win, qg, kdt, gcx, betax, gl)
    out = jax.ShapeDtypeStruct((bsz, s, DN_WIDTH), F32)
    return pl.pallas_call(
        functools.partial(_dn_scan_kernel, cpb=cpb, gl_rows=gl_rows),
        grid=(bsz, nb),
        in_specs=ins,
        out_specs=toks,
        out_shape=[out, out],
        scratch_shapes=[pltpu.VMEM((2, DN_HEADS // 2, LANES, LANES), F32)],
        compiler_params=_params(("parallel", "arbitrary")),
        name="dn_scan",
    )(*args, *args)


def _post_kernel(attn_ref, of_ref, ob_ref, dz_ref, g_ref, x_ref, gt1_ref, sh2_ref, sc2_ref,
                 dnw_ref, n2w_ref, bd_ref, wau_ref, wdu_ref, wo_ref, wrh_ref, wrl_ref,
                 x1_ref, h2_ref, afft_ref, *, parts):
    tm, d = x_ref.shape[1], x_ref.shape[2]
    rows = [slice(i * (tm // parts), (i + 1) * (tm // parts)) for i in range(parts)]
    o = [of_ref[0, r, :] + ob_ref[0, r, :] for r in rows]
    ms = [_dot((v * v).astype(BF16), bd_ref[...]) for v in o]
    dn = [(v * lax.rsqrt(m + EPS) * dnw_ref[...] * _silu(dz_ref[0, r, :].astype(F32))).astype(BF16)
          for v, m, r in zip(o, ms, rows)]
    up_a = [_dot(attn_ref[0, r, :], wau_ref[...]) for r in rows]
    up_d = [_dot(v, wdu_ref[...]) for v in dn]
    merged = [(jax.nn.sigmoid(g_ref[0, r, :d].astype(F32)) * a
               + jax.nn.sigmoid(g_ref[0, r, d:].astype(F32)) * b).astype(BF16)
              for r, a, b in zip(rows, up_a, up_d)]
    mixed = [_dot(v, wo_ref[...]) for v in merged]
    h2s = []
    for r, v in zip(rows, mixed):
        x1 = x_ref[0, r, :] + gt1_ref[0] * v
        x1_ref[0, r, :] = x1
        ms2 = jnp.mean(x1 * x1, axis=-1, keepdims=True)
        h2 = x1 * lax.rsqrt(ms2 + EPS) * n2w_ref[...]
        h2 = h2 * (1.0 + sc2_ref[0]) + sh2_ref[0]
        h2_ref[0, r, :] = h2.astype(BF16)
        h2s.append(h2)
    his = [v.astype(BF16) for v in h2s]
    los = [(v - h.astype(F32)).astype(BF16) for v, h in zip(h2s, his)]
    logits = [_dot(h, wrh_ref[...]) + _dot(l, wrh_ref[...]) + _dot(h, wrl_ref[...]) for h, l in zip(his, los)]
    for r, lg in zip(rows, logits):
        lane = lax.broadcasted_iota(jnp.int32, lg.shape, 1)
        lg = jnp.where(lane < N_EXPERTS, lg, -jnp.inf)
        e = jnp.exp(lg - jnp.max(lg, axis=1, keepdims=True))
        aff = e / jnp.sum(e, axis=1, keepdims=True)
        afft_ref[0, :, r] = aff.T[:N_EXPERTS]


def _post(attn, o_f, o_b, dz, graw, x, gt1, sh2, sc2, dnw, n2w, wau, wdu, wo, wr, tm):
    bsz, s, d = x.shape
    bd = _block_diag(DN_WIDTH, DN_HEAD_DIM, 1.0 / DN_HEAD_DIM)
    dnw_t = jnp.tile(dnw, DN_HEADS).reshape(1, DN_WIDTH)
    wr_p = jnp.zeros((d, LANES), F32).at[:, :N_EXPERTS].set(wr)
    wr_hi = wr_p.astype(BF16)
    wr_lo = (wr_p - wr_hi.astype(F32)).astype(BF16)
    tok = lambda w: pl.BlockSpec((1, tm, w), lambda b, i: (b, i, 0))
    full = lambda t: pl.BlockSpec(t.shape, lambda b, i: (0,) * t.ndim)
    vec = pl.BlockSpec((1, 1, d), lambda b, i: (b, 0, 0))
    sds = jax.ShapeDtypeStruct
    return pl.pallas_call(
        functools.partial(_post_kernel, parts=2 if tm % 256 == 0 else 1),
        grid=(bsz, s // tm),
        in_specs=[tok(ATTN_WIDTH), tok(DN_WIDTH), tok(DN_WIDTH), tok(DN_WIDTH), tok(2 * d), tok(d),
                  vec, vec, vec, full(dnw_t), full(n2w), full(bd), full(wau), full(wdu), full(wo),
                  full(wr_hi), full(wr_lo)],
        out_specs=[tok(d), tok(d), pl.BlockSpec((1, N_EXPERTS, tm), lambda b, i: (b, 0, i))],
        out_shape=[sds((bsz, s, d), F32), sds((bsz, s, d), BF16), sds((bsz, N_EXPERTS, s), F32)],
        compiler_params=_params(("parallel", "parallel")),
        name="post_mixer",
    )(attn, o_f, o_b, dz, graw, x, gt1, sh2, sc2, dnw_t, n2w, bd, wau, wdu, wo, wr_hi, wr_lo)


def _topk_kernel(aff_ref, tri_ref, code_ref, *, cap, s):
    a = aff_ref[0]
    ne = a.shape[0]
    count = lambda m: jnp.sum(jnp.where(m, 1.0, 0.0), axis=1, keepdims=True)

    def vbody(i, lo):
        cand = lo | (jnp.int32(1) << (30 - i))
        return jnp.where(count(a >= pltpu.bitcast(cand, F32)) >= cap, cand, lo)

    lo_bits = lax.fori_loop(0, 31, vbody, jnp.zeros((ne, 1), jnp.int32))
    lo = pltpu.bitcast(lo_bits, F32)
    hi = pltpu.bitcast(lo_bits + 1, F32)

    def rbody(i, lh):
        lo, hi = lh
        mid = 0.5 * (lo + hi)
        ok = count(a >= mid) >= cap
        return jnp.where(ok, mid, lo), jnp.where(ok, hi, mid)

    lo, hi = lax.fori_loop(0, 32, rbody, (lo, hi))
    gt = a >= hi
    eq = (a >= lo) & jnp.logical_not(gt)
    need = cap - count(gt)
    idx = lax.broadcasted_iota(jnp.int32, a.shape, 1)
    nbits = int(np.log2(s))

    def ibody(i, x):
        cand = x | (jnp.int32(1) << (nbits - 1 - i))
        return jnp.where(count(eq & (idx < cand)) < need, cand, x)

    last = lax.fori_loop(0, nbits, ibody, jnp.zeros((ne, 1), jnp.int32))
    sel = gt | (eq & (idx <= last))
    self32 = jnp.where(sel, 1.0, 0.0)
    tri = tri_ref[...]
    carry = jnp.zeros((ne, 1), F32)
    for t in range(s // LANES):
        seg = self32[:, t * LANES:(t + 1) * LANES]
        inc = _dot(seg.astype(BF16), tri)
        pos = (inc - seg + carry).astype(jnp.int32)
        code_ref[0, :, t * LANES:(t + 1) * LANES] = pos * 2 + seg.astype(jnp.int32)
        carry = carry + inc[:, LANES - 1:LANES]


def _topk(afft, cap):
    bsz, ne, s = afft.shape
    i = np.arange(LANES)
    tri = jnp.asarray(np.where(i[:, None] <= i[None, :], 1.0, 0.0), BF16)
    return pl.pallas_call(
        functools.partial(_topk_kernel, cap=cap, s=s),
        grid=(bsz,),
        in_specs=[pl.BlockSpec((1, ne, s), lambda b: (b, 0, 0)),
                  pl.BlockSpec((LANES, LANES), lambda b: (0, 0))],
        out_specs=pl.BlockSpec((1, ne, s), lambda b: (b, 0, 0)),
        out_shape=jax.ShapeDtypeStruct((bsz, ne, s), jnp.int32),
        compiler_params=_params(("parallel",)),
        name="topk",
    )(afft, tri)


def _moe_ffn_kernel(starts_ref, code_ref, aff_ref, h2_ref, wg_ref, wu_ref, wd_ref, y_ref, xe, gacc,
                    *, tb, win, cap, nj, nsub, spt):
    b, e, j = pl.program_id(0), pl.program_id(1), pl.program_id(2)

    @pl.when(j == 0)
    def _():
        xe[...] = jnp.zeros(xe.shape, F32)
        gacc[...] = jnp.zeros(gacc.shape, F32)

    code = code_ref[0, 0, 0]
    aff = aff_ref[0, 0, 0]
    base = ((b * pl.num_programs(1) + e) * nj + j) * nsub
    a = [pl.multiple_of(starts_ref[(base + k) * spt] // 16 * 16, 16) for k in range(nsub)]
    hot = []
    for k in range(nsub):
        r = lax.broadcasted_iota(jnp.int32, (win, tb), 0) + a[k]
        hot.append(code[:, k * tb:(k + 1) * tb] == 2 * r + 1)
    rows = [_dot(jnp.where(hot[k], 1.0, 0.0).astype(BF16), h2_ref[0, k * tb:(k + 1) * tb, :]) for k in range(nsub)]
    gates = [jnp.sum(jnp.where(hot[k], aff[:, k * tb:(k + 1) * tb], 0.0), axis=1, keepdims=True)
             for k in range(nsub)]
    for k in range(nsub):
        xe[pl.ds(a[k], win), :] = xe[pl.ds(a[k], win), :] + rows[k]
        gacc[pl.ds(a[k], win), :] = gacc[pl.ds(a[k], win), :] + gates[k]

    @pl.when(j == nj - 1)
    def _():
        xb = xe[0:cap, :].astype(BF16)
        hg = _dot(xb, wg_ref[0, 0].astype(BF16))
        hu = _dot(xb, wu_ref[0, 0].astype(BF16))
        act = (_silu(hg) * hu).astype(BF16)
        y_ref[0, 0, 0:cap, :] = (_dot(act, wd_ref[0, 0].astype(BF16)) * gacc[0:cap, :]).astype(BF16)
        y_ref[0, 0, cap:, :] = jnp.zeros((y_ref.shape[2] - cap, y_ref.shape[3]), BF16)


def _moe_ffn(starts, code, afft, h2, w_gate, w_up, w_down, cap, tb, win, capp, spt):
    bsz, s, d = h2.shape
    ne = code.shape[1]
    nsub = min(8, s // tb)
    tstep = nsub * tb
    nj = s // tstep
    f = w_gate.shape[-1]
    row = pl.BlockSpec((1, 1, 1, 1, tstep), lambda b, e, j, st: (b, e, j, 0, 0))
    wspec = lambda r, c: pl.BlockSpec((1, 1, r, c), lambda b, e, j, st: (0, e, 0, 0))
    return pl.pallas_call(
        functools.partial(_moe_ffn_kernel, tb=tb, win=win, cap=cap, nj=nj, nsub=nsub, spt=spt),
        grid_spec=pltpu.PrefetchScalarGridSpec(
            num_scalar_prefetch=1,
            grid=(bsz, ne, nj),
            in_specs=[row, row,
                      pl.BlockSpec((1, tstep, d), lambda b, e, j, st: (b, j, 0)),
                      wspec(d, f), wspec(d, f), wspec(f, d)],
            out_specs=pl.BlockSpec((1, 1, capp, d), lambda b, e, j, st: (b, e, 0, 0)),
            scratch_shapes=[pltpu.VMEM((capp, d), F32), pltpu.VMEM((capp, 1), F32)]),
        out_shape=jax.ShapeDtypeStruct((bsz, ne, capp, d), BF16),
        compiler_params=_params(("parallel", "parallel", "arbitrary")),
        name="moe_ffn",
    )(starts, code.reshape(bsz, ne, nj, 1, tstep), afft.reshape(bsz, ne, nj, 1, tstep), h2, w_gate, w_up, w_down)


TN_DIMS = (((0,), (0,)), ((), ()))


def _moe_scatter_kernel(starts_ref, code_ref, *refs, ne, tb, sub, swin, ns):
    y_refs = refs[:ne]
    x1_ref, gt2_ref, o_ref = refs[ne:]
    b, j = pl.program_id(0), pl.program_id(1)
    spb = tb // sub
    for k in range(spb):
        acc = None
        for e in range(ne):
            first = (b * ne + e) * ns + j * spb
            a_blk = starts_ref[first] // 16 * 16
            a_sub = starts_ref[first + k] // 16 * 16
            off = pl.multiple_of(a_sub - a_blk, 16)
            r = lax.broadcasted_iota(jnp.int32, (swin, sub), 0) + a_sub
            hot = code_ref[0, e:e + 1, k * sub:(k + 1) * sub] == 2 * r + 1
            z = lax.dot_general(jnp.where(hot, 1.0, 0.0).astype(BF16), y_refs[e][pl.ds(off, swin), :], TN_DIMS,
                                preferred_element_type=F32)
            acc = z if acc is None else acc + z
        rows = slice(k * sub, (k + 1) * sub)
        o_ref[0, rows, :] = x1_ref[0, rows, :] + gt2_ref[0] * acc


def _moe_scatter(starts, code, y, x1, gt2, tb, win, sub):
    bsz, s, d = x1.shape
    ne = code.shape[1]
    nj = s // tb
    ns = s // sub
    spb = tb // sub

    def window(e):
        return pl.BlockSpec((pl.squeezed, pl.squeezed, pl.Element(win), pl.Element(d)),
                            lambda b, j, st: (b, e, pl.multiple_of(st[(b * ne + e) * ns + j * spb] // 16 * 16, 16), 0))

    return pl.pallas_call(
        functools.partial(_moe_scatter_kernel, ne=ne, tb=tb, sub=sub, swin=sub + 16, ns=ns),
        grid_spec=pltpu.PrefetchScalarGridSpec(
            num_scalar_prefetch=1,
            grid=(bsz, nj),
            in_specs=[pl.BlockSpec((1, ne, tb), lambda b, j, st: (b, 0, j))]
                     + [window(e) for e in range(ne)]
                     + [pl.BlockSpec((1, tb, d), lambda b, j, st: (b, j, 0)),
                        pl.BlockSpec((1, 1, d), lambda b, j, st: (b, 0, 0))],
            out_specs=pl.BlockSpec((1, tb, d), lambda b, j, st: (b, j, 0))),
        out_shape=jax.ShapeDtypeStruct((bsz, s, d), F32),
        compiler_params=_params(("parallel", "parallel")),
        name="moe_scatter",
    )(starts, code, *([y] * ne), x1, gt2)


def _tile(s, pref):
    t = pref
    while s % t:
        t //= 2
    return t


def _layer(x, c, w_ada, b_ada, norm1_w, w_in, q_norm_w, k_norm_w, conv_w, a_log, dt_bias, dn_norm_w,
           w_attn_up, w_dn_up, w_o, norm2_w, w_router, w_gate, w_up, w_down):
    bsz, s, d = x.shape
    mod = _ada(c, w_ada, b_ada)
    sh1, sc1, gt1, sh2, sc2, gt2 = [m.reshape(bsz, 1, d) for m in jnp.split(mod, 6, axis=-1)]

    o0 = ATTN_WIDTH + 2 * KV_WIDTH
    o1 = o0 + 3 * DN_WIDTH
    o2 = o1 + DN_WIDTH
    o3 = o2 + 2 * DN_HEADS
    o4 = o3 + 2 * DN_HEADS
    pad = lambda w: jnp.zeros((d, LANES), w.dtype).at[:, :w.shape[1]].set(w)
    ws = [w_in[:, :o0], w_in[:, o0:o1], w_in[:, o1:o2], pad(w_in[:, o2:o3]), pad(w_in[:, o3:o4]), w_in[:, o4:]]
    ws = [w.astype(BF16) for w in ws]
    a_qkv, dqkv, dz, braw, araw, graw = _inproj(x, norm1_w.reshape(1, d), sh1, sc1, ws, _tile(s, 512))

    q, kt, v = _attn_prep(a_qkv, q_norm_w, k_norm_w, _tile(s, 512))
    attn = _attention(q, kt, v, _tile(s, 256), _tile(s, 1024))

    ts = 512
    prep = _dn_prep(dqkv, braw, araw, conv_w, a_log, dt_bias, ts)
    o_f, o_b = _dn_scan(prep, 4 * CHUNK, ts // CHUNK)

    x1, h2, afft = _post(attn, o_f, o_b, dz, graw, x, gt1, sh2, sc2, dn_norm_w, norm2_w.reshape(1, d),
                               w_attn_up.astype(BF16), w_dn_up.astype(BF16), w_o.astype(BF16), w_router,
                               _tile(s, 512))

    cap = CAPACITY_FACTOR * s // N_EXPERTS
    tb, sub = 256, 128
    win = tb + 16
    code = _topk(afft, cap)
    starts = (code[:, :, ::sub] >> 1).reshape(-1)
    y = _moe_ffn(starts, code, afft, h2, w_gate[None], w_up[None], w_down[None], cap, tb, win, cap + win, tb // sub)
    return _moe_scatter(starts, code, y, x1, gt2, tb, win, sub)


def kernel(x, c, w_ada, b_ada, norm1_w, w_in, q_norm_w, k_norm_w, conv_w, a_log, dt_bias, dn_norm_w,
           w_attn_up, w_dn_up, w_o, norm2_w, w_router, w_gate, w_up, w_down):
    depth = w_ada.shape[0]
    for l in range(depth):
        x = _layer(x, c, w_ada[l], b_ada[l], norm1_w[l], w_in[l], q_norm_w[l], k_norm_w[l], conv_w[l],
                   a_log[l], dt_bias[l], dn_norm_w[l], w_attn_up[l], w_dn_up[l], w_o[l], norm2_w[l],
                   w_router[l], w_gate[l], w_up[l], w_down[l])
    return x
```

```python
import functools

import numpy as np
import jax
import jax.numpy as jnp
from jax import lax
from jax.experimental import pallas as pl
from jax.experimental.pallas import tpu as pltpu

F32 = jnp.float32
BF16 = jnp.bfloat16

GRID_W = 64
N_Q_HEADS = 8
N_KV_HEADS = 2
HEAD_DIM = 64
ATTN_WIDTH = N_Q_HEADS * HEAD_DIM
KV_WIDTH = N_KV_HEADS * HEAD_DIM
ROPE_THETA = 10000.0
DN_HEADS = 8
DN_HEAD_DIM = 64
DN_WIDTH = DN_HEADS * DN_HEAD_DIM
CONV_WIDTH = 5
CHUNK = 64
N_EXPERTS = 16
CAPACITY_FACTOR = 2
EPS = 1e-6
LOG2E = 1.4426950408889634
LANES = 128
VMEM_LIMIT = 56 * 1024 * 1024

NT_DIMS = (((1,), (1,)), ((), ()))


def _params(sem):
    return pltpu.CompilerParams(dimension_semantics=sem, vmem_limit_bytes=VMEM_LIMIT)


def _dot(a, b):
    return jnp.dot(a, b, preferred_element_type=F32)


def _split_dot(x, m, parts):
    acc = None
    r = x
    for i in range(parts):
        h = r.astype(BF16)
        d = _dot(h, m)
        acc = d if acc is None else acc + d
        if i + 1 < parts:
            r = r - h.astype(F32)
    return acc


def _split_dot_l(m, x, parts):
    acc = None
    r = x
    for i in range(parts):
        h = r.astype(BF16)
        d = _dot(m, h)
        acc = d if acc is None else acc + d
        if i + 1 < parts:
            r = r - h.astype(F32)
    return acc


def _silu(x):
    return x * jax.nn.sigmoid(x)


def _ada_kernel(c_ref, w_ref, b_ref, o_ref):
    c = c_ref[...]
    o_ref[...] = jnp.dot(_silu(c), w_ref[...], preferred_element_type=F32,
                         precision=lax.Precision.HIGHEST) + b_ref[...]


def _ada(c, w, b):
    bsz, d = c.shape
    n = w.shape[1]
    tn = 1536
    rows = 8
    c8 = jnp.zeros((rows, d), F32).at[:bsz].set(c)
    out = pl.pallas_call(
        _ada_kernel,
        grid=(n // tn,),
        in_specs=[pl.BlockSpec((rows, d), lambda j: (0, 0)),
                  pl.BlockSpec((d, tn), lambda j: (0, j)),
                  pl.BlockSpec((1, tn), lambda j: (0, j))],
        out_specs=pl.BlockSpec((rows, tn), lambda j: (0, j)),
        out_shape=jax.ShapeDtypeStruct((rows, n), F32),
        compiler_params=_params(("arbitrary",)),
        name="ada",
    )(c8, w, b.reshape(1, n))
    return out[:bsz]


def _inproj_kernel(x_ref, nw_ref, sh_ref, sc_ref, wa_ref, wd_ref, wz_ref, wb_ref, wal_ref, wg_ref,
                   oa_ref, od_ref, oz_ref, ob_ref, oal_ref, og_ref):
    x = x_ref[0]
    ms = jnp.mean(x * x, axis=-1, keepdims=True)
    h = x * lax.rsqrt(ms + EPS) * nw_ref[...]
    h = h * (1.0 + sc_ref[0]) + sh_ref[0]
    hb = h.astype(BF16)
    oa_ref[0] = _dot(hb, wa_ref[...]).astype(BF16)
    od_ref[0] = _dot(hb, wd_ref[...]).astype(BF16)
    oz_ref[0] = _dot(hb, wz_ref[...]).astype(BF16)
    ob_ref[0] = _dot(hb, wb_ref[...])
    oal_ref[0] = _dot(hb, wal_ref[...])
    og_ref[0] = _dot(hb, wg_ref[...]).astype(BF16)


def _inproj(x, nw, sh, sc, ws, tm):
    bsz, s, d = x.shape
    widths = [w.shape[1] for w in ws]
    dts = [BF16, BF16, BF16, F32, F32, BF16]
    tok = lambda w: pl.BlockSpec((1, tm, w), lambda b, i: (b, i, 0))
    full = lambda a: pl.BlockSpec(a.shape, lambda b, i: (0,) * a.ndim)
    vec = pl.BlockSpec((1, 1, d), lambda b, i: (b, 0, 0))
    return pl.pallas_call(
        _inproj_kernel,
        grid=(bsz, s // tm),
        in_specs=[tok(d), full(nw), vec, vec] + [full(w) for w in ws],
        out_specs=[tok(w) for w in widths],
        out_shape=[jax.ShapeDtypeStruct((bsz, s, w), dt) for w, dt in zip(widths, dts)],
        compiler_params=_params(("parallel", "parallel")),
        name="inproj",
    )(x, nw, sh, sc, *ws)


def _norm_rope(x, w, bd, cos, sa, sb):
    n = x.shape[1]
    ms = _dot((x * x).astype(BF16), bd)
    xn = x * lax.rsqrt(ms + EPS) * w
    return xn * cos + pltpu.roll(xn, n - 16, 1) * sa + pltpu.roll(xn, 16, 1) * sb


def _attn_prep_kernel(a_ref, cos_ref, sa_ref, sb_ref, qw_ref, kw_ref, bdq_ref, bdk_ref,
                      q_ref, kt_ref, v_ref):
    a = a_ref[0].astype(F32)
    q = a[:, :ATTN_WIDTH]
    k = a[:, ATTN_WIDTH:ATTN_WIDTH + KV_WIDTH]
    v = a[:, ATTN_WIDTH + KV_WIDTH:]
    cos, sa, sb = cos_ref[...], sa_ref[...], sb_ref[...]
    rep = ATTN_WIDTH // LANES
    tile = lambda t: jnp.concatenate([t] * rep, axis=1)
    qr = _norm_rope(q, qw_ref[...], bdq_ref[...], tile(cos), tile(sa), tile(sb))
    q_ref[0] = (qr * (HEAD_DIM ** -0.5 * LOG2E)).astype(BF16)
    kr = _norm_rope(k, kw_ref[...], bdk_ref[...], cos, sa, sb)
    kt = kr.T
    kt_ref[0, 0] = kt[:HEAD_DIM].astype(BF16)
    kt_ref[0, 1] = kt[HEAD_DIM:].astype(BF16)
    lane = lax.broadcasted_iota(jnp.int32, v.shape, 1)
    ones_col = jnp.where(lane == HEAD_DIM, 1.0, 0.0)
    v_ref[0, 0] = jnp.where(lane < HEAD_DIM, v, ones_col).astype(BF16)
    v_ref[0, 1] = jnp.where(lane < HEAD_DIM, pltpu.roll(v, HEAD_DIM, 1), ones_col).astype(BF16)


def _rope_tables(s):
    pos = jnp.arange(s)
    lane = jnp.arange(LANES)
    d = lane % HEAD_DIM
    p = d % 32
    f = (p % 16).astype(F32)
    freqs = ROPE_THETA ** (-(f * 2.0 / 32.0))
    axis_pos = jnp.where((d // 32)[None, :] == 0, (pos // GRID_W)[:, None], (pos % GRID_W)[:, None])
    ang = axis_pos.astype(F32) * freqs[None, :]
    cos, sin = jnp.cos(ang), jnp.sin(ang)
    first = (p < 16)[None, :]
    return cos, jnp.where(first, -sin, 0.0), jnp.where(first, 0.0, sin)


def _block_diag(n, blk, val):
    i = np.arange(n)
    return jnp.asarray(np.where((i[:, None] // blk) == (i[None, :] // blk), val, 0.0), BF16)


def _attn_prep(a, qw, kw, tp):
    bsz, s, wa = a.shape
    cos, sa, sb = _rope_tables(s)
    bdq = _block_diag(ATTN_WIDTH, HEAD_DIM, 1.0 / HEAD_DIM)
    bdk = _block_diag(KV_WIDTH, HEAD_DIM, 1.0 / HEAD_DIM)
    qw_t = jnp.tile(qw, N_Q_HEADS).reshape(1, ATTN_WIDTH)
    kw_t = jnp.tile(kw, N_KV_HEADS).reshape(1, KV_WIDTH)
    tab = pl.BlockSpec((tp, LANES), lambda b, i: (i, 0))
    full = lambda t: pl.BlockSpec(t.shape, lambda b, i: (0,) * t.ndim)
    return pl.pallas_call(
        _attn_prep_kernel,
        grid=(bsz, s // tp),
        in_specs=[pl.BlockSpec((1, tp, wa), lambda b, i: (b, i, 0)), tab, tab, tab,
                  full(qw_t), full(kw_t), full(bdq), full(bdk)],
        out_specs=[pl.BlockSpec((1, tp, ATTN_WIDTH), lambda b, i: (b, i, 0)),
                   pl.BlockSpec((1, N_KV_HEADS, HEAD_DIM, tp), lambda b, i: (b, 0, 0, i)),
                   pl.BlockSpec((1, N_KV_HEADS, tp, LANES), lambda b, i: (b, 0, i, 0))],
        out_shape=[jax.ShapeDtypeStruct((bsz, s, ATTN_WIDTH), BF16),
                   jax.ShapeDtypeStruct((bsz, N_KV_HEADS, HEAD_DIM, s), BF16),
                   jax.ShapeDtypeStruct((bsz, N_KV_HEADS, s, LANES), BF16)],
        compiler_params=_params(("parallel", "parallel")),
        name="attn_prep",
    )(a, cos, sa, sb, qw_t, kw_t, bdq, bdk)


def _attn_kernel(q_ref, kt_ref, v_ref, o_ref, *, tq, tk, group):
    rows = group * tq
    q = q_ref[0]
    qs = jnp.concatenate([q[:, h * HEAD_DIM:(h + 1) * HEAD_DIM] for h in range(group)], axis=0)
    nk = kt_ref.shape[3] // tk
    scores = lambda k: _dot(qs, kt_ref[0, 0, :, k * tk:(k + 1) * tk])
    m = jnp.full((rows, LANES), -jnp.inf, F32)
    acc = jnp.zeros((rows, LANES), F32)
    s_next = scores(0)
    for k in range(nk):
        s = s_next
        if k + 1 < nk:
            s_next = scores(k + 1)
        m_new = jnp.maximum(m, jnp.max(s, axis=1, keepdims=True))
        alpha = jnp.exp2(m - m_new)
        p = jnp.concatenate([jnp.exp2(s[:, t * LANES:(t + 1) * LANES] - m_new).astype(BF16)
                             for t in range(tk // LANES)], axis=1)
        acc = alpha * acc + _dot(p, v_ref[0, 0, k * tk:(k + 1) * tk, :])
        m = m_new
    o = acc[:, :HEAD_DIM] * (1.0 / acc[:, HEAD_DIM:HEAD_DIM + 1])
    o_ref[0] = jnp.concatenate([o[h * tq:(h + 1) * tq] for h in range(group)], axis=1).astype(BF16)


def _attention(q, kt, v, tq, tk):
    bsz, s, _ = q.shape
    group = N_Q_HEADS // N_KV_HEADS
    gw = group * HEAD_DIM
    return pl.pallas_call(
        functools.partial(_attn_kernel, tq=tq, tk=tk, group=group),
        grid=(bsz, N_KV_HEADS, s // tq),
        in_specs=[pl.BlockSpec((1, tq, gw), lambda b, g, i: (b, i, g)),
                  pl.BlockSpec((1, 1, HEAD_DIM, s), lambda b, g, i: (b, g, 0, 0)),
                  pl.BlockSpec((1, 1, s, LANES), lambda b, g, i: (b, g, 0, 0))],
        out_specs=pl.BlockSpec((1, tq, gw), lambda b, g, i: (b, i, g)),
        out_shape=jax.ShapeDtypeStruct((bsz, s, ATTN_WIDTH), BF16),
        compiler_params=_params(("parallel", "parallel", "parallel")),
        name="attention",
    )(q, kt, v)


def _dn_prep_kernel(cur_ref, prev_ref, next_ref, cw_ref, braw_ref, araw_ref, alog_ref, dtb_ref,
                    bd_ref, trif_ref, trib_ref, sel_ref, exf_ref, exb_ref,
                    kn_ref, qn_ref, vb_ref, win_ref, qg_ref, kdt_ref, gcx_ref, betax_ref, gl_ref,
                    *, ts):
    i = pl.program_id(1)
    halo = prev_ref.shape[1]
    ext = jnp.concatenate([jnp.where(i > 0, prev_ref[0].astype(F32), 0.0),
                           cur_ref[0].astype(F32),
                           jnp.where(i < pl.num_programs(1) - 1, next_ref[0].astype(F32), 0.0)], axis=0)
    cw = cw_ref[...]
    conv = None
    for j in range(CONV_WIDTH):
        shift = (CONV_WIDTH // 2 - j) % ext.shape[0]
        tap = (pltpu.roll(ext, shift, 0) if shift else ext)[halo:halo + ts] * cw[j:j + 1, :]
        conv = tap if conv is None else conv + tap
    act = _silu(conv)
    cq, ck, cv = act[:, :DN_WIDTH], act[:, DN_WIDTH:2 * DN_WIDTH], act[:, 2 * DN_WIDTH:]
    bd = bd_ref[...]
    qn = cq * lax.rsqrt(_dot((cq * cq).astype(BF16), bd) + EPS) * (DN_HEAD_DIM ** -0.5)
    kn = ck * lax.rsqrt(_dot((ck * ck).astype(BF16), bd) + EPS)
    kn_ref[0] = kn.astype(BF16)
    qn_ref[0] = qn.astype(BF16)

    beta = jax.nn.sigmoid(braw_ref[0])
    g = -jnp.exp(alog_ref[...]) * jax.nn.softplus(araw_ref[0] + dtb_ref[...])
    gl = jnp.exp(_split_dot_l(sel_ref[...], g, 3))
    gcs = [_split_dot_l(trif_ref[...], g, 3), _split_dot_l(trib_ref[...], g, 3)]
    for d, ex_ref in enumerate((exf_ref, exb_ref)):
        gc = gcs[d]
        ex = ex_ref[...]
        beta_x = _dot(beta.astype(BF16), ex)
        gc_x = _split_dot(gc, ex, 3)
        ek_x = _dot(jnp.exp(gcs[1 - d] - g).astype(BF16), ex)
        eg_x = jnp.exp(gc_x)
        vb_ref[d, 0] = (cv * beta_x).astype(BF16)
        win_ref[d, 0] = (kn * beta_x * eg_x).astype(BF16)
        qg_ref[d, 0] = (qn * eg_x).astype(BF16)
        kdt_ref[d, 0] = (kn * ek_x).T.astype(BF16)
        gcx_ref[d, 0] = gc_x
        betax_ref[d, 0] = beta_x
        gl_ref[d, 0] = _split_dot(gl, ex, 2)


def _dn_prep(dqkv, braw, araw, conv_w, a_log, dt_bias, ts):
    bsz, s, w3 = dqkv.shape
    nt = s // ts
    cpt = ts // CHUNK
    assert cpt == 8
    halo = 16
    hb = ts // halo
    idx = np.arange(ts)
    same = (idx[:, None] // CHUNK) == (idx[None, :] // CHUNK)
    trif = jnp.asarray(np.where(same & (idx[:, None] >= idx[None, :]), 1.0, 0.0), BF16)
    trib = jnp.asarray(np.where(same & (idx[:, None] <= idx[None, :]), 1.0, 0.0), BF16)
    sel = jnp.asarray(np.where(np.arange(cpt)[:, None] == (idx[None, :] // CHUNK), 1.0, 0.0), BF16)
    bd = _block_diag(DN_WIDTH, DN_HEAD_DIM, 1.0)
    lane = np.arange(DN_WIDTH) // DN_HEAD_DIM
    row = np.arange(LANES)
    exf = jnp.asarray(np.where(row[:, None] == lane[None, :], 1.0, 0.0), BF16)
    exb = jnp.asarray(np.where(row[:, None] == lane[None, :] + DN_HEADS, 1.0, 0.0), BF16)
    nh2 = 2 * DN_HEADS
    alog = jnp.zeros((1, LANES), F32).at[0, :nh2].set(a_log.reshape(nh2))
    dtb = jnp.zeros((1, LANES), F32).at[0, :nh2].set(dt_bias.reshape(nh2))
    full = lambda t: pl.BlockSpec(t.shape, lambda b, i: (0,) * t.ndim)
    tok = lambda w: pl.BlockSpec((1, ts, w), lambda b, i: (b, i, 0))
    dtok = lambda w: pl.BlockSpec((2, 1, ts, w), lambda b, i: (0, b, i, 0))
    sds = jax.ShapeDtypeStruct
    return pl.pallas_call(
        functools.partial(_dn_prep_kernel, ts=ts),
        grid=(bsz, nt),
        in_specs=[tok(w3),
                  pl.BlockSpec((1, halo, w3), lambda b, i: (b, jnp.maximum(i * hb - 1, 0), 0)),
                  pl.BlockSpec((1, halo, w3), lambda b, i: (b, jnp.minimum((i + 1) * hb, s // halo - 1), 0)),
                  full(conv_w), tok(LANES), tok(LANES), full(alog), full(dtb),
                  full(bd), full(trif), full(trib), full(sel), full(exf), full(exb)],
        out_specs=[tok(DN_WIDTH), tok(DN_WIDTH), dtok(DN_WIDTH), dtok(DN_WIDTH), dtok(DN_WIDTH),
                   pl.BlockSpec((2, 1, DN_WIDTH, ts), lambda b, i: (0, b, 0, i)),
                   dtok(DN_WIDTH), dtok(DN_WIDTH),
                   pl.BlockSpec((2, 1, cpt, DN_WIDTH), lambda b, i: (0, b, i, 0))],
        out_shape=[sds((bsz, s, DN_WIDTH), BF16), sds((bsz, s, DN_WIDTH), BF16),
                   sds((2, bsz, s, DN_WIDTH), BF16), sds((2, bsz, s, DN_WIDTH), BF16),
                   sds((2, bsz, s, DN_WIDTH), BF16), sds((2, bsz, DN_WIDTH, s), BF16),
                   sds((2, bsz, s, DN_WIDTH), F32), sds((2, bsz, s, DN_WIDTH), F32),
                   sds((2, bsz, s // CHUNK, DN_WIDTH), F32)],
        compiler_params=_params(("parallel", "parallel")),
        name="dn_prep",
    )(dqkv, dqkv, dqkv, conv_w, braw, araw, alog, dtb, bd, trif, trib, sel, exf, exb)


def _pair_diag(x):
    left = lax.broadcasted_iota(jnp.int32, x.shape, 1) < DN_HEAD_DIM
    zero = jnp.zeros_like(x)
    return jnp.concatenate([jnp.where(left, x, zero), jnp.where(left, zero, x)], axis=0)


def _dn_scan_kernel(*refs, cpb, gl_rows):
    nin = 9
    ins = (refs[:nin], refs[nin:2 * nin])
    outs = refs[2 * nin:2 * nin + 2]
    state = refs[2 * nin + 2]
    j = pl.program_id(1)
    nb = pl.num_programs(1)

    @pl.when(j == 0)
    def _():
        state[...] = jnp.zeros(state.shape, F32)

    ri = lax.broadcasted_iota(jnp.int32, (CHUNK, LANES), 0)
    ci = lax.broadcasted_iota(jnp.int32, (CHUNK, LANES), 1) % CHUNK
    incl = (ri >= ci, ri <= ci)
    strict = (ri > ci, ri < ci)
    diag = ri == ci
    eye = jnp.where(diag, 1.0, 0.0)
    merge = [((ri // (2 * sz)) == (ci // (2 * sz))) & ((ri // sz) != (ci // sz))
             for sz in (1, 2, 4, 8, 16, 32)]
    rd = lax.broadcasted_iota(jnp.int32, (LANES, LANES), 0) // DN_HEAD_DIM
    cd = lax.broadcasted_iota(jnp.int32, (LANES, LANES), 1) // DN_HEAD_DIM
    same_head = rd == cd

    dirs = (0, 1)
    pairs = range(DN_HEADS // 2)
    items = [(d, c, p) for d in dirs for c in range(cpb) for p in pairs]
    rows = [slice(c * CHUNK, (c + 1) * CHUNK) for c in range(cpb)]
    ls = [slice(p * LANES, (p + 1) * LANES) for p in pairs]
    kq = {(d, c, p): jnp.concatenate([ins[d][0][0, rows[c], ls[p]], ins[d][1][0, rows[c], ls[p]]], axis=0)
          for d, c, p in items}
    a = {i: lax.dot_general(kq[i], _pair_diag(kq[i][:CHUNK]), NT_DIMS, preferred_element_type=F32)
         for i in items}
    dec, lm, t = {}, {}, {}
    for d, c, p in items:
        gcx = ins[d][6][0, 0, rows[c], ls[p]]
        grow = jnp.sum(jnp.where(diag, gcx, 0.0), axis=0, keepdims=True)
        i = (d, c, p)
        dec[i] = jnp.exp(jnp.where(incl[d], gcx - grow, -jnp.inf))
        lm[i] = jnp.where(strict[d], a[i][:CHUNK] * ins[d][7][0, 0, rows[c], ls[p]] * dec[i], 0.0)
        t[i] = eye - jnp.where(merge[0], lm[i], 0.0)
    for mk in merge[1:]:
        x = {i: _dot(jnp.where(mk, lm[i], 0.0).astype(BF16), _pair_diag(t[i].astype(BF16))).astype(BF16)
             for i in items}
        t = {i: t[i] - _dot(t[i].astype(BF16), _pair_diag(x[i])) for i in items}
    uw = {(d, c, p): _dot(t[d, c, p].astype(BF16),
                          jnp.concatenate([_pair_diag(ins[d][2][0, 0, rows[c], ls[p]]),
                                           _pair_diag(ins[d][3][0, 0, rows[c], ls[p]])], axis=1))
          for d, c, p in items}
    a_in = {i: (a[i][CHUNK:] * dec[i]).astype(BF16) for i in items}

    zeros = jnp.zeros((CHUNK, LANES), BF16)
    blk = (j, nb - 1 - j)
    for step in range(cpb):
        chunk = (step, cpb - 1 - step)
        live = [(d, chunk[d], p) for d in dirs for p in pairs]
        gl_c = [ins[d][8][0, 0, pl.ds((blk[d] * cpb) % gl_rows + chunk[d], 1), :] for d in dirs]
        s_old = {(d, p): state[d, p] for d in dirs for p in pairs}
        wq = {(d, c, p): _dot(jnp.concatenate([uw[d, c, p][:, LANES:].astype(BF16),
                                               ins[d][4][0, 0, rows[c], ls[p]]], axis=0),
                              s_old[d, p].astype(BF16)) for d, c, p in live}
        vnb = {i: (uw[i][:, :LANES] - wq[i][:CHUNK]).astype(BF16) for i in live}
        o_in = {i: _dot(a_in[i], _pair_diag(vnb[i])) for i in live}
        ds = {(d, c, p): _dot(ins[d][5][0, 0, ls[p], (c // 2) * LANES:(c // 2 + 1) * LANES],
                              jnp.concatenate([vnb[d, c, p], zeros] if c % 2 == 0 else [zeros, vnb[d, c, p]], axis=0))
              for d, c, p in live}
        for d, c, p in live:
            state[d, p] = s_old[d, p] * gl_c[d][:, ls[p]] + jnp.where(same_head, ds[d, c, p], 0.0)
            outs[d][0, rows[c], ls[p]] = wq[d, c, p][CHUNK:] + o_in[d, c, p]


def _dn_scan(prep, tsb, gl_rows):
    kn, qn, vb, win, qg, kdt, gcx, betax, gl = prep
    bsz, s, _ = kn.shape
    nb = s // tsb
    cpb = tsb // CHUNK
    assert cpb % 2 == 0
    toks, ins = [], []
    for d in (0, 1):
        blk = (lambda j: j) if d == 0 else (lambda j: nb - 1 - j)
        tok = pl.BlockSpec((1, tsb, DN_WIDTH), lambda b, j, blk=blk: (b, blk(j), 0))
        dtok = pl.BlockSpec((1, 1, tsb, DN_WIDTH), lambda b, j, blk=blk, d=d: (d, b, blk(j), 0))
        toks.append(tok)
        ins += [tok, tok, dtok, dtok, dtok,
                pl.BlockSpec((1, 1, DN_WIDTH, tsb), lambda b, j, blk=blk, d=d: (d, b, 0, blk(j))),
                dtok, dtok,
                pl.BlockSpec((1, 1, gl_rows, DN_WIDTH), lambda b, j, blk=blk, d=d: (d, b, (blk(j) * cpb) // gl_rows, 0))]
    args = (kn, qn, vb, win, qg, kdt, gcx, betax, gl)
    out = jax.ShapeDtypeStruct((bsz, s, DN_WIDTH), F32)
    return pl.pallas_call(
        functools.partial(_dn_scan_kernel, cpb=cpb, gl_rows=gl_rows),
        grid=(bsz, nb),
        in_specs=ins,
        out_specs=toks,
        out_shape=[out, out],
        scratch_shapes=[pltpu.VMEM((2, DN_HEADS // 2, LANES, LANES), F32)],
        compiler_params=_params(("parallel", "arbitrary")),
        name="dn_scan",
    )(*args, *args)


def _post_kernel(attn_ref, of_ref, ob_ref, dz_ref, g_ref, x_ref, gt1_ref, sh2_ref, sc2_ref,
                 dnw_ref, n2w_ref, bd_ref, wau_ref, wdu_ref, wo_ref, wrh_ref, wrl_ref,
                 x1_ref, h2_ref, afft_ref, *, parts):
    tm, d = x_ref.shape[1], x_ref.shape[2]
    rows = [slice(i * (tm // parts), (i + 1) * (tm // parts)) for i in range(parts)]
    o = [of_ref[0, r, :] + ob_ref[0, r, :] for r in rows]
    ms = [_dot((v * v).astype(BF16), bd_ref[...]) for v in o]
    dn = [(v * lax.rsqrt(m + EPS) * dnw_ref[...] * _silu(dz_ref[0, r, :].astype(F32))).astype(BF16)
          for v, m, r in zip(o, ms, rows)]
    up_a = [_dot(attn_ref[0, r, :], wau_ref[...]) for r in rows]
    up_d = [_dot(v, wdu_ref[...]) for v in dn]
    merged = [(jax.nn.sigmoid(g_ref[0, r, :d].astype(F32)) * a
               + jax.nn.sigmoid(g_ref[0, r, d:].astype(F32)) * b).astype(BF16)
              for r, a, b in zip(rows, up_a, up_d)]
    mixed = [_dot(v, wo_ref[...]) for v in merged]
    h2s = []
    for r, v in zip(rows, mixed):
        x1 = x_ref[0, r, :] + gt1_ref[0] * v
        x1_ref[0, r, :] = x1
        ms2 = jnp.mean(x1 * x1, axis=-1, keepdims=True)
        h2 = x1 * lax.rsqrt(ms2 + EPS) * n2w_ref[...]
        h2 = h2 * (1.0 + sc2_ref[0]) + sh2_ref[0]
        h2_ref[0, r, :] = h2.astype(BF16)
        h2s.append(h2)
    his = [v.astype(BF16) for v in h2s]
    los = [(v - h.astype(F32)).astype(BF16) for v, h in zip(h2s, his)]
    logits = [_dot(h, wrh_ref[...]) + _dot(l, wrh_ref[...]) + _dot(h, wrl_ref[...]) for h, l in zip(his, los)]
    for r, lg in zip(rows, logits):
        lane = lax.broadcasted_iota(jnp.int32, lg.shape, 1)
        lg = jnp.where(lane < N_EXPERTS, lg, -jnp.inf)
        e = jnp.exp(lg - jnp.max(lg, axis=1, keepdims=True))
        aff = e / jnp.sum(e, axis=1, keepdims=True)
        afft_ref[0, :, r] = aff.T[:N_EXPERTS]


def _post(attn, o_f, o_b, dz, graw, x, gt1, sh2, sc2, dnw, n2w, wau, wdu, wo, wr, tm):
    bsz, s, d = x.shape
    bd = _block_diag(DN_WIDTH, DN_HEAD_DIM, 1.0 / DN_HEAD_DIM)
    dnw_t = jnp.tile(dnw, DN_HEADS).reshape(1, DN_WIDTH)
    wr_p = jnp.zeros((d, LANES), F32).at[:, :N_EXPERTS].set(wr)
    wr_hi = wr_p.astype(BF16)
    wr_lo = (wr_p - wr_hi.astype(F32)).astype(BF16)
    tok = lambda w: pl.BlockSpec((1, tm, w), lambda b, i: (b, i, 0))
    full = lambda t: pl.BlockSpec(t.shape, lambda b, i: (0,) * t.ndim)
    vec = pl.BlockSpec((1, 1, d), lambda b, i: (b, 0, 0))
    sds = jax.ShapeDtypeStruct
    return pl.pallas_call(
        functools.partial(_post_kernel, parts=2 if tm % 256 == 0 else 1),
        grid=(bsz, s // tm),
        in_specs=[tok(ATTN_WIDTH), tok(DN_WIDTH), tok(DN_WIDTH), tok(DN_WIDTH), tok(2 * d), tok(d),
                  vec, vec, vec, full(dnw_t), full(n2w), full(bd), full(wau), full(wdu), full(wo),
                  full(wr_hi), full(wr_lo)],
        out_specs=[tok(d), tok(d), pl.BlockSpec((1, N_EXPERTS, tm), lambda b, i: (b, 0, i))],
        out_shape=[sds((bsz, s, d), F32), sds((bsz, s, d), BF16), sds((bsz, N_EXPERTS, s), F32)],
        compiler_params=_params(("parallel", "parallel")),
        name="post_mixer",
    )(attn, o_f, o_b, dz, graw, x, gt1, sh2, sc2, dnw_t, n2w, bd, wau, wdu, wo, wr_hi, wr_lo)


def _topk_kernel(aff_ref, tri_ref, code_ref, *, cap, s):
    a = aff_ref[0]
    ne = a.shape[0]
    count = lambda m: jnp.sum(jnp.where(m, 1.0, 0.0), axis=1, keepdims=True)

    def vbody(i, lo):
        cand = lo | (jnp.int32(1) << (30 - i))
        return jnp.where(count(a >= pltpu.bitcast(cand, F32)) >= cap, cand, lo)

    lo_bits = lax.fori_loop(0, 31, vbody, jnp.zeros((ne, 1), jnp.int32))
    lo = pltpu.bitcast(lo_bits, F32)
    hi = pltpu.bitcast(lo_bits + 1, F32)

    def rbody(i, lh):
        lo, hi = lh
        mid = 0.5 * (lo + hi)
        ok = count(a >= mid) >= cap
        return jnp.where(ok, mid, lo), jnp.where(ok, hi, mid)

    lo, hi = lax.fori_loop(0, 32, rbody, (lo, hi))
    gt = a >= hi
    eq = (a >= lo) & jnp.logical_not(gt)
    need = cap - count(gt)
    idx = lax.broadcasted_iota(jnp.int32, a.shape, 1)
    nbits = int(np.log2(s))

    def ibody(i, x):
        cand = x | (jnp.int32(1) << (nbits - 1 - i))
        return jnp.where(count(eq & (idx < cand)) < need, cand, x)

    last = lax.fori_loop(0, nbits, ibody, jnp.zeros((ne, 1), jnp.int32))
    sel = gt | (eq & (idx <= last))
    self32 = jnp.where(sel, 1.0, 0.0)
    tri = tri_ref[...]
    carry = jnp.zeros((ne, 1), F32)
    for t in range(s // LANES):
        seg = self32[:, t * LANES:(t + 1) * LANES]
        inc = _dot(seg.astype(BF16), tri)
        pos = (inc - seg + carry).astype(jnp.int32)
        code_ref[0, :, t * LANES:(t + 1) * LANES] = pos * 2 + seg.astype(jnp.int32)
        carry = carry + inc[:, LANES - 1:LANES]


def _topk(afft, cap):
    bsz, ne, s = afft.shape
    i = np.arange(LANES)
    tri = jnp.asarray(np.where(i[:, None] <= i[None, :], 1.0, 0.0), BF16)
    return pl.pallas_call(
        functools.partial(_topk_kernel, cap=cap, s=s),
        grid=(bsz,),
        in_specs=[pl.BlockSpec((1, ne, s), lambda b: (b, 0, 0)),
                  pl.BlockSpec((LANES, LANES), lambda b: (0, 0))],
        out_specs=pl.BlockSpec((1, ne, s), lambda b: (b, 0, 0)),
        out_shape=jax.ShapeDtypeStruct((bsz, ne, s), jnp.int32),
        compiler_params=_params(("parallel",)),
        name="topk",
    )(afft, tri)


def _moe_gather_kernel(starts_ref, ends_ref, code_ref, aff_ref, h2_ref, xe_ref, gate_ref,
                       *, ne, tb, win, fwin, nsub, spt, eg, ns):
    b, g, j = pl.program_id(0), pl.program_id(1), pl.program_id(2)

    @pl.when(j == 0)
    def _():
        xe_ref[...] = jnp.zeros(xe_ref.shape, BF16)
        gate_ref[...] = jnp.zeros(gate_ref.shape, F32)

    experts = [g * eg + i for i in range(eg)]
    for k in range(nsub):
        idx = [(b * ne + experts[i]) * ns + (j * nsub + k) * spt for i in range(eg)]
        a = [pl.multiple_of(starts_ref[idx[i]] // 16 * 16, 16) for i in range(eg)]
        fits = functools.reduce(jnp.logical_and, [ends_ref[idx[i] + spt - 1] - a[i] <= fwin for i in range(eg)])
        cols = slice(k * tb, (k + 1) * tb)

        def emit(n, a=a, cols=cols):
            hot = [code_ref[0, pl.ds(experts[i], 1), cols]
                   == 2 * (lax.broadcasted_iota(jnp.int32, (n, tb), 0) + a[i]) + 1 for i in range(eg)]
            rows = _dot(jnp.concatenate([jnp.where(h, 1.0, 0.0).astype(BF16) for h in hot], axis=0),
                        h2_ref[0, cols, :])
            for i in range(eg):
                sl = pl.ds(a[i], n)
                xe_ref[0, i, sl, :] = xe_ref[0, i, sl, :] + rows[i * n:(i + 1) * n].astype(BF16)
                gsum = jnp.sum(jnp.where(hot[i], aff_ref[0, pl.ds(experts[i], 1), cols], 0.0), axis=1, keepdims=True)
                gate_ref[0, i, sl, :] = gate_ref[0, i, sl, :] + gsum

        pl.when(fits)(functools.partial(emit, fwin))
        pl.when(jnp.logical_not(fits))(functools.partial(emit, win))


def _moe_gather(starts, ends, code, afft, h2, tb, win, fwin, capp, spt):
    bsz, s, d = h2.shape
    ne = code.shape[1]
    eg = 4
    nsub = min(4, s // tb)
    tstep = nsub * tb
    nj = s // tstep
    ns = s // tb * spt
    rows = pl.BlockSpec((1, ne, tstep), lambda b, g, j, st, en: (b, 0, j))
    return pl.pallas_call(
        functools.partial(_moe_gather_kernel, ne=ne, tb=tb, win=win, fwin=fwin, nsub=nsub, spt=spt, eg=eg, ns=ns),
        grid_spec=pltpu.PrefetchScalarGridSpec(
            num_scalar_prefetch=2,
            grid=(bsz, ne // eg, nj),
            in_specs=[rows, rows, pl.BlockSpec((1, tstep, d), lambda b, g, j, st, en: (b, j, 0))],
            out_specs=[pl.BlockSpec((1, eg, capp, d), lambda b, g, j, st, en: (b, g, 0, 0)),
                       pl.BlockSpec((1, eg, capp, 1), lambda b, g, j, st, en: (b, g, 0, 0))]),
        out_shape=[jax.ShapeDtypeStruct((bsz, ne, capp, d), BF16),
                   jax.ShapeDtypeStruct((bsz, ne, capp, 1), F32)],
        compiler_params=_params(("parallel", "parallel", "arbitrary")),
        name="moe_gather",
    )(starts, ends, code, afft, h2)


def _moe_ffn_kernel(xe_ref, gate_ref, wg_ref, wu_ref, wd_ref, y_ref, *, cap):
    xb = xe_ref[0, 0]
    hg = _dot(xb, wg_ref[0, 0].astype(BF16))
    hu = _dot(xb, wu_ref[0, 0].astype(BF16))
    act = (_silu(hg) * hu).astype(BF16)
    y_ref[0, 0, 0:cap, :] = (_dot(act, wd_ref[0, 0].astype(BF16)) * gate_ref[0, 0]).astype(BF16)
    y_ref[0, 0, cap:, :] = jnp.zeros((y_ref.shape[2] - cap, y_ref.shape[3]), BF16)


def _moe_ffn(xe, gates, w_gate, w_up, w_down, cap, capp):
    bsz, ne, _, d = xe.shape
    f = w_gate.shape[-1]
    wspec = lambda r, c: pl.BlockSpec((1, 1, r, c), lambda b, e: (0, e, 0, 0))
    return pl.pallas_call(
        functools.partial(_moe_ffn_kernel, cap=cap),
        grid=(bsz, ne),
        in_specs=[pl.BlockSpec((1, 1, cap, d), lambda b, e: (b, e, 0, 0)),
                  pl.BlockSpec((1, 1, cap, 1), lambda b, e: (b, e, 0, 0)),
                  wspec(d, f), wspec(d, f), wspec(f, d)],
        out_specs=pl.BlockSpec((1, 1, capp, d), lambda b, e: (b, e, 0, 0)),
        out_shape=jax.ShapeDtypeStruct((bsz, ne, capp, d), BF16),
        compiler_params=_params(("parallel", "parallel")),
        name="moe_ffn",
    )(xe, gates, w_gate, w_up, w_down)


TN_DIMS = (((0,), (0,)), ((), ()))


def _moe_scatter_kernel(starts_ref, ends_ref, code_ref, *refs, ne, tb, sub, swin, fwin, ns, wstep):
    y_refs = refs[:ne]
    x1_ref, gt2_ref, o_ref, ycat = refs[ne:]
    b, j = pl.program_id(0), pl.program_id(1)
    spb = tb // sub
    for k in range(spb):
        first = [(b * ne + e) * ns + j * spb for e in range(ne)]
        a_sub = [starts_ref[first[e] + k] // 16 * 16 for e in range(ne)]
        off = [pl.multiple_of(a_sub[e] - starts_ref[first[e]] // wstep * wstep, 16) for e in range(ne)]
        fits = functools.reduce(jnp.logical_and, [ends_ref[first[e] + k] - a_sub[e] <= fwin for e in range(ne)])
        rows = slice(k * sub, (k + 1) * sub)
        hot = lambda e, n: (code_ref[0, e:e + 1, rows] == 2 * (lax.broadcasted_iota(jnp.int32, (n, sub), 0)
                                                               + a_sub[e]) + 1)

        @pl.when(fits)
        def _():
            for e in range(ne):
                ycat[e * fwin:(e + 1) * fwin, :] = y_refs[e][pl.ds(off[e], fwin), :]
            sel = jnp.concatenate([jnp.where(hot(e, fwin), 1.0, 0.0).astype(BF16) for e in range(ne)], axis=0)
            acc = lax.dot_general(sel, ycat[...], TN_DIMS, preferred_element_type=F32)
            o_ref[0, rows, :] = x1_ref[0, rows, :] + gt2_ref[0] * acc

        @pl.when(jnp.logical_not(fits))
        def _():
            acc = None
            for e in range(ne):
                z = lax.dot_general(jnp.where(hot(e, swin), 1.0, 0.0).astype(BF16), y_refs[e][pl.ds(off[e], swin), :],
                                    TN_DIMS, preferred_element_type=F32)
                acc = z if acc is None else acc + z
            o_ref[0, rows, :] = x1_ref[0, rows, :] + gt2_ref[0] * acc


def _moe_scatter(starts, ends, code, y, x1, gt2, tb, win, sub, fwin, wstep):
    bsz, s, d = x1.shape
    ne = code.shape[1]
    nj = s // tb
    ns = s // sub
    spb = tb // sub

    def window(e):
        return pl.BlockSpec((pl.squeezed, pl.squeezed, pl.Element(win), pl.Element(d)),
                            lambda b, j, st, en: (b, e, pl.multiple_of(st[(b * ne + e) * ns + j * spb] // wstep * wstep,
                                                                       wstep), 0))

    return pl.pallas_call(
        functools.partial(_moe_scatter_kernel, ne=ne, tb=tb, sub=sub, swin=sub + 16, fwin=fwin, ns=ns, wstep=wstep),
        grid_spec=pltpu.PrefetchScalarGridSpec(
            num_scalar_prefetch=2,
            grid=(bsz, nj),
            in_specs=[pl.BlockSpec((1, ne, tb), lambda b, j, st, en: (b, 0, j))]
                     + [window(e) for e in range(ne)]
                     + [pl.BlockSpec((1, tb, d), lambda b, j, st, en: (b, j, 0)),
                        pl.BlockSpec((1, 1, d), lambda b, j, st, en: (b, 0, 0))],
            out_specs=pl.BlockSpec((1, tb, d), lambda b, j, st, en: (b, j, 0)),
            scratch_shapes=[pltpu.VMEM((ne * fwin, d), BF16)]),
        out_shape=jax.ShapeDtypeStruct((bsz, s, d), F32),
        compiler_params=_params(("parallel", "parallel")),
        name="moe_scatter",
    )(starts, ends, code, *([y] * ne), x1, gt2)


def _tile(s, pref):
    t = pref
    while s % t:
        t //= 2
    return t


def _layer(x, c, w_ada, b_ada, norm1_w, w_in, q_norm_w, k_norm_w, conv_w, a_log, dt_bias, dn_norm_w,
           w_attn_up, w_dn_up, w_o, norm2_w, w_router, w_gate, w_up, w_down):
    bsz, s, d = x.shape
    mod = _ada(c, w_ada, b_ada)
    sh1, sc1, gt1, sh2, sc2, gt2 = [m.reshape(bsz, 1, d) for m in jnp.split(mod, 6, axis=-1)]

    o0 = ATTN_WIDTH + 2 * KV_WIDTH
    o1 = o0 + 3 * DN_WIDTH
    o2 = o1 + DN_WIDTH
    o3 = o2 + 2 * DN_HEADS
    o4 = o3 + 2 * DN_HEADS
    pad = lambda w: jnp.zeros((d, LANES), w.dtype).at[:, :w.shape[1]].set(w)
    ws = [w_in[:, :o0], w_in[:, o0:o1], w_in[:, o1:o2], pad(w_in[:, o2:o3]), pad(w_in[:, o3:o4]), w_in[:, o4:]]
    ws = [w.astype(BF16) for w in ws]
    a_qkv, dqkv, dz, braw, araw, graw = _inproj(x, norm1_w.reshape(1, d), sh1, sc1, ws, _tile(s, 512))

    q, kt, v = _attn_prep(a_qkv, q_norm_w, k_norm_w, _tile(s, 512))
    attn = _attention(q, kt, v, _tile(s, 256), _tile(s, 1024))

    ts = 512
    prep = _dn_prep(dqkv, braw, araw, conv_w, a_log, dt_bias, ts)
    o_f, o_b = _dn_scan(prep, 4 * CHUNK, ts // CHUNK)

    x1, h2, afft = _post(attn, o_f, o_b, dz, graw, x, gt1, sh2, sc2, dn_norm_w, norm2_w.reshape(1, d),
                               w_attn_up.astype(BF16), w_dn_up.astype(BF16), w_o.astype(BF16), w_router,
                               _tile(s, 512))

    cap = CAPACITY_FACTOR * s // N_EXPERTS
    tb, sub = 256, 128
    win = tb + 16
    code = _topk(afft, cap)
    counts = code[:, :, ::sub] >> 1
    starts = counts.reshape(-1)
    ends = jnp.concatenate([counts[:, :, 1:], jnp.full_like(counts[:, :, :1], cap)], axis=2).reshape(-1)
    wstep = 128
    ywin = wstep + tb + 16
    gwin = 80
    xe, gates = _moe_gather(starts, ends, code, afft, h2, tb, win, gwin, cap + win, tb // sub)
    y = _moe_ffn(xe, gates, w_gate[None], w_up[None], w_down[None], cap, cap + ywin)
    fwin = 48
    return _moe_scatter(starts, ends, code, y, x1, gt2, tb, ywin, sub, fwin, wstep)


def kernel(x, c, w_ada, b_ada, norm1_w, w_in, q_norm_w, k_norm_w, conv_w, a_log, dt_bias, dn_norm_w,
           w_attn_up, w_dn_up, w_o, norm2_w, w_router, w_gate, w_up, w_down):
    depth = w_ada.shape[0]
    for l in range(depth):
        x = _layer(x, c, w_ada[l], b_ada[l], norm1_w[l], w_in[l], q_norm_w[l], k_norm_w[l], conv_w[l],
                   a_log[l], dt_bias[l], dn_norm_w[l], w_attn_up[l], w_dn_up[l], w_o[l], norm2_w[l],
                   w_router[l], w_gate[l], w_up[l], w_down[l])
    return x
```

```python
import functools

import numpy as np
import jax
import jax.numpy as jnp
from jax import lax
from jax.experimental import pallas as pl
from jax.experimental.pallas import tpu as pltpu

F32 = jnp.float32
BF16 = jnp.bfloat16

GRID_W = 64
N_Q_HEADS = 8
N_KV_HEADS = 2
HEAD_DIM = 64
ATTN_WIDTH = N_Q_HEADS * HEAD_DIM
KV_WIDTH = N_KV_HEADS * HEAD_DIM
ROPE_THETA = 10000.0
DN_HEADS = 8
DN_HEAD_DIM = 64
DN_WIDTH = DN_HEADS * DN_HEAD_DIM
CONV_WIDTH = 5
CHUNK = 64
N_EXPERTS = 16
CAPACITY_FACTOR = 2
EPS = 1e-6
LOG2E = 1.4426950408889634
LANES = 128
VMEM_LIMIT = 56 * 1024 * 1024

NT_DIMS = (((1,), (1,)), ((), ()))


def _params(sem):
    return pltpu.CompilerParams(dimension_semantics=sem, vmem_limit_bytes=VMEM_LIMIT)


def _dot(a, b):
    return jnp.dot(a, b, preferred_element_type=F32)


def _split_dot(x, m, parts):
    acc = None
    r = x
    for i in range(parts):
        h = r.astype(BF16)
        d = _dot(h, m)
        acc = d if acc is None else acc + d
        if i + 1 < parts:
            r = r - h.astype(F32)
    return acc


def _split_dot_l(m, x, parts):
    acc = None
    r = x
    for i in range(parts):
        h = r.astype(BF16)
        d = _dot(m, h)
        acc = d if acc is None else acc + d
        if i + 1 < parts:
            r = r - h.astype(F32)
    return acc


def _silu(x):
    return x * jax.nn.sigmoid(x)


def _ada_kernel(c_ref, w_ref, b_ref, o_ref):
    c = c_ref[...]
    o_ref[...] = jnp.dot(_silu(c), w_ref[...], preferred_element_type=F32,
                         precision=lax.Precision.HIGHEST) + b_ref[...]


def _ada(c, w, b):
    bsz, d = c.shape
    n = w.shape[1]
    tn = 1536
    rows = 8
    c8 = jnp.zeros((rows, d), F32).at[:bsz].set(c)
    out = pl.pallas_call(
        _ada_kernel,
        grid=(n // tn,),
        in_specs=[pl.BlockSpec((rows, d), lambda j: (0, 0)),
                  pl.BlockSpec((d, tn), lambda j: (0, j)),
                  pl.BlockSpec((1, tn), lambda j: (0, j))],
        out_specs=pl.BlockSpec((rows, tn), lambda j: (0, j)),
        out_shape=jax.ShapeDtypeStruct((rows, n), F32),
        compiler_params=_params(("arbitrary",)),
        name="ada",
    )(c8, w, b.reshape(1, n))
    return out[:bsz]


def _norm_rope(x, w, bd, cos, sa, sb):
    n = x.shape[1]
    ms = _dot((x * x).astype(BF16), bd)
    xn = x * lax.rsqrt(ms + EPS) * w
    return xn * cos + pltpu.roll(xn, n - 16, 1) * sa + pltpu.roll(xn, 16, 1) * sb


def _inproj_kernel(x_ref, nw_ref, sh_ref, sc_ref, wa_ref, wd_ref, wz_ref, wb_ref, wal_ref, wg_ref,
                   cos_ref, sa_ref, sb_ref, qw_ref, kw_ref, bdq_ref, bdk_ref,
                   q_ref, kt_ref, v_ref, od_ref, oz_ref, ob_ref, oal_ref, og_ref):
    x = x_ref[0]
    ms = jnp.mean(x * x, axis=-1, keepdims=True)
    h = x * lax.rsqrt(ms + EPS) * nw_ref[...]
    h = h * (1.0 + sc_ref[0]) + sh_ref[0]
    hb = h.astype(BF16)
    a = _dot(hb, wa_ref[...])
    od_ref[0] = _dot(hb, wd_ref[...]).astype(BF16)
    oz_ref[0] = _dot(hb, wz_ref[...]).astype(BF16)
    ob_ref[0] = _dot(hb, wb_ref[...])
    oal_ref[0] = _dot(hb, wal_ref[...])
    og_ref[0] = _dot(hb, wg_ref[...]).astype(BF16)

    q = a[:, :ATTN_WIDTH]
    k = a[:, ATTN_WIDTH:ATTN_WIDTH + KV_WIDTH]
    v = a[:, ATTN_WIDTH + KV_WIDTH:]
    cos, sa, sb = cos_ref[...], sa_ref[...], sb_ref[...]
    rep = ATTN_WIDTH // LANES
    tile = lambda t: jnp.concatenate([t] * rep, axis=1)
    qr = _norm_rope(q, qw_ref[...], bdq_ref[...], tile(cos), tile(sa), tile(sb))
    q_ref[0] = (qr * (HEAD_DIM ** -0.5 * LOG2E)).astype(BF16)
    kr = _norm_rope(k, kw_ref[...], bdk_ref[...], cos, sa, sb)
    kt = kr.T
    kt_ref[0, 0] = kt[:HEAD_DIM].astype(BF16)
    kt_ref[0, 1] = kt[HEAD_DIM:].astype(BF16)
    lane = lax.broadcasted_iota(jnp.int32, v.shape, 1)
    ones_col = jnp.where(lane == HEAD_DIM, 1.0, 0.0)
    v_ref[0, 0] = jnp.where(lane < HEAD_DIM, v, ones_col).astype(BF16)
    v_ref[0, 1] = jnp.where(lane < HEAD_DIM, pltpu.roll(v, HEAD_DIM, 1), ones_col).astype(BF16)


def _rope_tables(s):
    pos = jnp.arange(s)
    lane = jnp.arange(LANES)
    d = lane % HEAD_DIM
    p = d % 32
    f = (p % 16).astype(F32)
    freqs = ROPE_THETA ** (-(f * 2.0 / 32.0))
    axis_pos = jnp.where((d // 32)[None, :] == 0, (pos // GRID_W)[:, None], (pos % GRID_W)[:, None])
    ang = axis_pos.astype(F32) * freqs[None, :]
    cos, sin = jnp.cos(ang), jnp.sin(ang)
    first = (p < 16)[None, :]
    return cos, jnp.where(first, -sin, 0.0), jnp.where(first, 0.0, sin)


def _block_diag(n, blk, val):
    i = np.arange(n)
    return jnp.asarray(np.where((i[:, None] // blk) == (i[None, :] // blk), val, 0.0), BF16)


def _inproj(x, nw, sh, sc, ws, qw, kw, tm):
    bsz, s, d = x.shape
    widths = [w.shape[1] for w in ws[1:]]
    dts = [BF16, BF16, F32, F32, BF16]
    cos, sa, sb = _rope_tables(s)
    bdq = _block_diag(ATTN_WIDTH, HEAD_DIM, 1.0 / HEAD_DIM)
    bdk = _block_diag(KV_WIDTH, HEAD_DIM, 1.0 / HEAD_DIM)
    qw_t = jnp.tile(qw, N_Q_HEADS).reshape(1, ATTN_WIDTH)
    kw_t = jnp.tile(kw, N_KV_HEADS).reshape(1, KV_WIDTH)
    tok = lambda w: pl.BlockSpec((1, tm, w), lambda b, i: (b, i, 0))
    full = lambda a: pl.BlockSpec(a.shape, lambda b, i: (0,) * a.ndim)
    vec = pl.BlockSpec((1, 1, d), lambda b, i: (b, 0, 0))
    tab = pl.BlockSpec((tm, LANES), lambda b, i: (i, 0))
    sds = jax.ShapeDtypeStruct
    return pl.pallas_call(
        _inproj_kernel,
        grid=(bsz, s // tm),
        in_specs=[tok(d), full(nw), vec, vec] + [full(w) for w in ws]
                 + [tab, tab, tab, full(qw_t), full(kw_t), full(bdq), full(bdk)],
        out_specs=[tok(ATTN_WIDTH),
                   pl.BlockSpec((1, N_KV_HEADS, HEAD_DIM, tm), lambda b, i: (b, 0, 0, i)),
                   pl.BlockSpec((1, N_KV_HEADS, tm, LANES), lambda b, i: (b, 0, i, 0))]
                  + [tok(w) for w in widths],
        out_shape=[sds((bsz, s, ATTN_WIDTH), BF16), sds((bsz, N_KV_HEADS, HEAD_DIM, s), BF16),
                   sds((bsz, N_KV_HEADS, s, LANES), BF16)]
                  + [sds((bsz, s, w), dt) for w, dt in zip(widths, dts)],
        compiler_params=_params(("parallel", "parallel")),
        name="inproj",
    )(x, nw, sh, sc, *ws, cos, sa, sb, qw_t, kw_t, bdq, bdk)


def _attn_kernel(q_ref, kt_ref, v_ref, o_ref, *, tq, tk, group):
    rows = group * tq
    q = q_ref[0]
    qs = jnp.concatenate([q[:, h * HEAD_DIM:(h + 1) * HEAD_DIM] for h in range(group)], axis=0)
    nk = kt_ref.shape[3] // tk
    scores = lambda k: _dot(qs, kt_ref[0, 0, :, k * tk:(k + 1) * tk])
    m = jnp.full((rows, LANES), -jnp.inf, F32)
    acc = jnp.zeros((rows, LANES), F32)
    s_next = scores(0)
    for k in range(nk):
        s = s_next
        if k + 1 < nk:
            s_next = scores(k + 1)
        m_new = jnp.maximum(m, jnp.max(s, axis=1, keepdims=True))
        alpha = jnp.exp2(m - m_new)
        p = jnp.concatenate([jnp.exp2(s[:, t * LANES:(t + 1) * LANES] - m_new).astype(BF16)
                             for t in range(tk // LANES)], axis=1)
        acc = alpha * acc + _dot(p, v_ref[0, 0, k * tk:(k + 1) * tk, :])
        m = m_new
    o = acc[:, :HEAD_DIM] * (1.0 / acc[:, HEAD_DIM:HEAD_DIM + 1])
    o_ref[0] = jnp.concatenate([o[h * tq:(h + 1) * tq] for h in range(group)], axis=1).astype(BF16)


def _attention(q, kt, v, tq, tk):
    bsz, s, _ = q.shape
    group = N_Q_HEADS // N_KV_HEADS
    gw = group * HEAD_DIM
    return pl.pallas_call(
        functools.partial(_attn_kernel, tq=tq, tk=tk, group=group),
        grid=(bsz, N_KV_HEADS, s // tq),
        in_specs=[pl.BlockSpec((1, tq, gw), lambda b, g, i: (b, i, g)),
                  pl.BlockSpec((1, 1, HEAD_DIM, s), lambda b, g, i: (b, g, 0, 0)),
                  pl.BlockSpec((1, 1, s, LANES), lambda b, g, i: (b, g, 0, 0))],
        out_specs=pl.BlockSpec((1, tq, gw), lambda b, g, i: (b, i, g)),
        out_shape=jax.ShapeDtypeStruct((bsz, s, ATTN_WIDTH), BF16),
        compiler_params=_params(("parallel", "parallel", "parallel")),
        name="attention",
    )(q, kt, v)


def _dn_prep_kernel(cur_ref, prev_ref, next_ref, cw_ref, braw_ref, araw_ref, alog_ref, dtb_ref,
                    bd_ref, trif_ref, trib_ref, sel_ref, exf_ref, exb_ref,
                    kn_ref, qn_ref, vb_ref, win_ref, qg_ref, kdt_ref, gcx_ref, betax_ref, gl_ref,
                    *, ts):
    i = pl.program_id(1)
    halo = prev_ref.shape[1]
    ext = jnp.concatenate([jnp.where(i > 0, prev_ref[0].astype(F32), 0.0),
                           cur_ref[0].astype(F32),
                           jnp.where(i < pl.num_programs(1) - 1, next_ref[0].astype(F32), 0.0)], axis=0)
    cw = cw_ref[...]
    conv = None
    for j in range(CONV_WIDTH):
        shift = (CONV_WIDTH // 2 - j) % ext.shape[0]
        tap = (pltpu.roll(ext, shift, 0) if shift else ext)[halo:halo + ts] * cw[j:j + 1, :]
        conv = tap if conv is None else conv + tap
    act = _silu(conv)
    cq, ck, cv = act[:, :DN_WIDTH], act[:, DN_WIDTH:2 * DN_WIDTH], act[:, 2 * DN_WIDTH:]
    bd = bd_ref[...]
    qn = cq * lax.rsqrt(_dot((cq * cq).astype(BF16), bd) + EPS) * (DN_HEAD_DIM ** -0.5)
    kn = ck * lax.rsqrt(_dot((ck * ck).astype(BF16), bd) + EPS)
    kn_ref[0] = kn.astype(BF16)
    qn_ref[0] = qn.astype(BF16)

    beta = jax.nn.sigmoid(braw_ref[0])
    g = -jnp.exp(alog_ref[...]) * jax.nn.softplus(araw_ref[0] + dtb_ref[...])
    gl = jnp.exp(_split_dot_l(sel_ref[...], g, 3))
    gcs = [_split_dot_l(trif_ref[...], g, 3), _split_dot_l(trib_ref[...], g, 3)]
    for d, ex_ref in enumerate((exf_ref, exb_ref)):
        gc = gcs[d]
        ex = ex_ref[...]
        beta_x = _dot(beta.astype(BF16), ex)
        gc_x = _split_dot(gc, ex, 3)
        ek_x = _dot(jnp.exp(gcs[1 - d] - g).astype(BF16), ex)
        eg_x = jnp.exp(gc_x)
        vb_ref[d, 0] = (cv * beta_x).astype(BF16)
        win_ref[d, 0] = (kn * beta_x * eg_x).astype(BF16)
        qg_ref[d, 0] = (qn * eg_x).astype(BF16)
        kdt_ref[d, 0] = (kn * ek_x).T.astype(BF16)
        gcx_ref[d, 0] = gc_x
        betax_ref[d, 0] = beta_x
        gl_ref[d, 0] = _split_dot(gl, ex, 2)


def _dn_prep(dqkv, braw, araw, conv_w, a_log, dt_bias, ts):
    bsz, s, w3 = dqkv.shape
    nt = s // ts
    cpt = ts // CHUNK
    assert cpt == 8
    halo = 16
    hb = ts // halo
    idx = np.arange(ts)
    same = (idx[:, None] // CHUNK) == (idx[None, :] // CHUNK)
    trif = jnp.asarray(np.where(same & (idx[:, None] >= idx[None, :]), 1.0, 0.0), BF16)
    trib = jnp.asarray(np.where(same & (idx[:, None] <= idx[None, :]), 1.0, 0.0), BF16)
    sel = jnp.asarray(np.where(np.arange(cpt)[:, None] == (idx[None, :] // CHUNK), 1.0, 0.0), BF16)
    bd = _block_diag(DN_WIDTH, DN_HEAD_DIM, 1.0)
    lane = np.arange(DN_WIDTH) // DN_HEAD_DIM
    row = np.arange(LANES)
    exf = jnp.asarray(np.where(row[:, None] == lane[None, :], 1.0, 0.0), BF16)
    exb = jnp.asarray(np.where(row[:, None] == lane[None, :] + DN_HEADS, 1.0, 0.0), BF16)
    nh2 = 2 * DN_HEADS
    alog = jnp.zeros((1, LANES), F32).at[0, :nh2].set(a_log.reshape(nh2))
    dtb = jnp.zeros((1, LANES), F32).at[0, :nh2].set(dt_bias.reshape(nh2))
    full = lambda t: pl.BlockSpec(t.shape, lambda b, i: (0,) * t.ndim)
    tok = lambda w: pl.BlockSpec((1, ts, w), lambda b, i: (b, i, 0))
    dtok = lambda w: pl.BlockSpec((2, 1, ts, w), lambda b, i: (0, b, i, 0))
    sds = jax.ShapeDtypeStruct
    return pl.pallas_call(
        functools.partial(_dn_prep_kernel, ts=ts),
        grid=(bsz, nt),
        in_specs=[tok(w3),
                  pl.BlockSpec((1, halo, w3), lambda b, i: (b, jnp.maximum(i * hb - 1, 0), 0)),
                  pl.BlockSpec((1, halo, w3), lambda b, i: (b, jnp.minimum((i + 1) * hb, s // halo - 1), 0)),
                  full(conv_w), tok(LANES), tok(LANES), full(alog), full(dtb),
                  full(bd), full(trif), full(trib), full(sel), full(exf), full(exb)],
        out_specs=[tok(DN_WIDTH), tok(DN_WIDTH), dtok(DN_WIDTH), dtok(DN_WIDTH), dtok(DN_WIDTH),
                   pl.BlockSpec((2, 1, DN_WIDTH, ts), lambda b, i: (0, b, 0, i)),
                   dtok(DN_WIDTH), dtok(DN_WIDTH),
                   pl.BlockSpec((2, 1, cpt, DN_WIDTH), lambda b, i: (0, b, i, 0))],
        out_shape=[sds((bsz, s, DN_WIDTH), BF16), sds((bsz, s, DN_WIDTH), BF16),
                   sds((2, bsz, s, DN_WIDTH), BF16), sds((2, bsz, s, DN_WIDTH), BF16),
                   sds((2, bsz, s, DN_WIDTH), BF16), sds((2, bsz, DN_WIDTH, s), BF16),
                   sds((2, bsz, s, DN_WIDTH), F32), sds((2, bsz, s, DN_WIDTH), F32),
                   sds((2, bsz, s // CHUNK, DN_WIDTH), F32)],
        compiler_params=_params(("parallel", "parallel")),
        name="dn_prep",
    )(dqkv, dqkv, dqkv, conv_w, braw, araw, alog, dtb, bd, trif, trib, sel, exf, exb)


def _pair_diag(x):
    left = lax.broadcasted_iota(jnp.int32, x.shape, 1) < DN_HEAD_DIM
    zero = jnp.zeros_like(x)
    return jnp.concatenate([jnp.where(left, x, zero), jnp.where(left, zero, x)], axis=0)


def _dn_scan_kernel(*refs, cpb, gl_rows):
    nin = 9
    ins = (refs[:nin], refs[nin:2 * nin])
    outs = refs[2 * nin:2 * nin + 2]
    state = refs[2 * nin + 2]
    j = pl.program_id(1)
    nb = pl.num_programs(1)

    @pl.when(j == 0)
    def _():
        state[...] = jnp.zeros(state.shape, F32)

    ri = lax.broadcasted_iota(jnp.int32, (CHUNK, LANES), 0)
    ci = lax.broadcasted_iota(jnp.int32, (CHUNK, LANES), 1) % CHUNK
    incl = (ri >= ci, ri <= ci)
    strict = (ri > ci, ri < ci)
    diag = ri == ci
    eye = jnp.where(diag, 1.0, 0.0)
    merge = [((ri // (2 * sz)) == (ci // (2 * sz))) & ((ri // sz) != (ci // sz))
             for sz in (1, 2, 4, 8, 16, 32)]
    rd = lax.broadcasted_iota(jnp.int32, (LANES, LANES), 0) // DN_HEAD_DIM
    cd = lax.broadcasted_iota(jnp.int32, (LANES, LANES), 1) // DN_HEAD_DIM
    same_head = rd == cd

    dirs = (0, 1)
    pairs = range(DN_HEADS // 2)
    items = [(d, c, p) for d in dirs for c in range(cpb) for p in pairs]
    rows = [slice(c * CHUNK, (c + 1) * CHUNK) for c in range(cpb)]
    ls = [slice(p * LANES, (p + 1) * LANES) for p in pairs]
    kq = {(d, c, p): jnp.concatenate([ins[d][0][0, rows[c], ls[p]], ins[d][1][0, rows[c], ls[p]]], axis=0)
          for d, c, p in items}
    a = {i: lax.dot_general(kq[i], _pair_diag(kq[i][:CHUNK]), NT_DIMS, preferred_element_type=F32)
         for i in items}
    dec, lm, t = {}, {}, {}
    for d, c, p in items:
        gcx = ins[d][6][0, 0, rows[c], ls[p]]
        grow = jnp.sum(jnp.where(diag, gcx, 0.0), axis=0, keepdims=True)
        i = (d, c, p)
        dec[i] = jnp.exp(jnp.where(incl[d], gcx - grow, -jnp.inf))
        lm[i] = jnp.where(strict[d], a[i][:CHUNK] * ins[d][7][0, 0, rows[c], ls[p]] * dec[i], 0.0)
        t[i] = eye - jnp.where(merge[0], lm[i], 0.0)
    for mk in merge[1:]:
        x = {i: _dot(jnp.where(mk, lm[i], 0.0).astype(BF16), _pair_diag(t[i].astype(BF16))).astype(BF16)
             for i in items}
        t = {i: t[i] - _dot(t[i].astype(BF16), _pair_diag(x[i])) for i in items}
    uw = {(d, c, p): _dot(t[d, c, p].astype(BF16),
                          jnp.concatenate([_pair_diag(ins[d][2][0, 0, rows[c], ls[p]]),
                                           _pair_diag(ins[d][3][0, 0, rows[c], ls[p]])], axis=1))
          for d, c, p in items}
    a_in = {i: (a[i][CHUNK:] * dec[i]).astype(BF16) for i in items}

    zeros = jnp.zeros((CHUNK, LANES), BF16)
    blk = (j, nb - 1 - j)
    for step in range(cpb):
        chunk = (step, cpb - 1 - step)
        live = [(d, chunk[d], p) for d in dirs for p in pairs]
        gl_c = [ins[d][8][0, 0, pl.ds((blk[d] * cpb) % gl_rows + chunk[d], 1), :] for d in dirs]
        s_old = {(d, p): state[d, p] for d in dirs for p in pairs}
        wq = {(d, c, p): _dot(jnp.concatenate([uw[d, c, p][:, LANES:].astype(BF16),
                                               ins[d][4][0, 0, rows[c], ls[p]]], axis=0),
                              s_old[d, p].astype(BF16)) for d, c, p in live}
        vnb = {i: (uw[i][:, :LANES] - wq[i][:CHUNK]).astype(BF16) for i in live}
        o_in = {i: _dot(a_in[i], _pair_diag(vnb[i])) for i in live}
        ds = {(d, c, p): _dot(ins[d][5][0, 0, ls[p], (c // 2) * LANES:(c // 2 + 1) * LANES],
                              jnp.concatenate([vnb[d, c, p], zeros] if c % 2 == 0 else [zeros, vnb[d, c, p]], axis=0))
              for d, c, p in live}
        for d, c, p in live:
            state[d, p] = s_old[d, p] * gl_c[d][:, ls[p]] + jnp.where(same_head, ds[d, c, p], 0.0)
            outs[d][0, rows[c], ls[p]] = wq[d, c, p][CHUNK:] + o_in[d, c, p]


def _dn_scan(prep, tsb, gl_rows):
    kn, qn, vb, win, qg, kdt, gcx, betax, gl = prep
    bsz, s, _ = kn.shape
    nb = s // tsb
    cpb = tsb // CHUNK
    assert cpb % 2 == 0
    toks, ins = [], []
    for d in (0, 1):
        blk = (lambda j: j) if d == 0 else (lambda j: nb - 1 - j)
        tok = pl.BlockSpec((1, tsb, DN_WIDTH), lambda b, j, blk=blk: (b, blk(j), 0))
        dtok = pl.BlockSpec((1, 1, tsb, DN_WIDTH), lambda b, j, blk=blk, d=d: (d, b, blk(j), 0))
        toks.append(tok)
        ins += [tok, tok, dtok, dtok, dtok,
                pl.BlockSpec((1, 1, DN_WIDTH, tsb), lambda b, j, blk=blk, d=d: (d, b, 0, blk(j))),
                dtok, dtok,
                pl.BlockSpec((1, 1, gl_rows, DN_WIDTH), lambda b, j, blk=blk, d=d: (d, b, (blk(j) * cpb) // gl_rows, 0))]
    args = (kn, qn, vb, win, qg, kdt, gcx, betax, gl)
    out = jax.ShapeDtypeStruct((bsz, s, DN_WIDTH), F32)
    return pl.pallas_call(
        functools.partial(_dn_scan_kernel, cpb=cpb, gl_rows=gl_rows),
        grid=(bsz, nb),
        in_specs=ins,
        out_specs=toks,
        out_shape=[out, out],
        scratch_shapes=[pltpu.VMEM((2, DN_HEADS // 2, LANES, LANES), F32)],
        compiler_params=_params(("parallel", "arbitrary")),
        name="dn_scan",
    )(*args, *args)


def _post_kernel(attn_ref, of_ref, ob_ref, dz_ref, g_ref, x_ref, gt1_ref, sh2_ref, sc2_ref,
                 dnw_ref, n2w_ref, bd_ref, wau_ref, wdu_ref, wo_ref, wrh_ref, wrl_ref,
                 x1_ref, h2_ref, afft_ref, *, parts):
    tm, d = x_ref.shape[1], x_ref.shape[2]
    rows = [slice(i * (tm // parts), (i + 1) * (tm // parts)) for i in range(parts)]
    o = [of_ref[0, r, :] + ob_ref[0, r, :] for r in rows]
    ms = [_dot((v * v).astype(BF16), bd_ref[...]) for v in o]
    dn = [(v * lax.rsqrt(m + EPS) * dnw_ref[...] * _silu(dz_ref[0, r, :].astype(F32))).astype(BF16)
          for v, m, r in zip(o, ms, rows)]
    up_a = [_dot(attn_ref[0, r, :], wau_ref[...]) for r in rows]
    up_d = [_dot(v, wdu_ref[...]) for v in dn]
    merged = [(jax.nn.sigmoid(g_ref[0, r, :d].astype(F32)) * a
               + jax.nn.sigmoid(g_ref[0, r, d:].astype(F32)) * b).astype(BF16)
              for r, a, b in zip(rows, up_a, up_d)]
    mixed = [_dot(v, wo_ref[...]) for v in merged]
    h2s = []
    for r, v in zip(rows, mixed):
        x1 = x_ref[0, r, :] + gt1_ref[0] * v
        x1_ref[0, r, :] = x1
        ms2 = jnp.mean(x1 * x1, axis=-1, keepdims=True)
        h2 = x1 * lax.rsqrt(ms2 + EPS) * n2w_ref[...]
        h2 = h2 * (1.0 + sc2_ref[0]) + sh2_ref[0]
        h2_ref[0, r, :] = h2.astype(BF16)
        h2s.append(h2)
    his = [v.astype(BF16) for v in h2s]
    los = [(v - h.astype(F32)).astype(BF16) for v, h in zip(h2s, his)]
    logits = [_dot(h, wrh_ref[...]) + _dot(l, wrh_ref[...]) + _dot(h, wrl_ref[...]) for h, l in zip(his, los)]
    for r, lg in zip(rows, logits):
        lane = lax.broadcasted_iota(jnp.int32, lg.shape, 1)
        lg = jnp.where(lane < N_EXPERTS, lg, -jnp.inf)
        e = jnp.exp(lg - jnp.max(lg, axis=1, keepdims=True))
        aff = e / jnp.sum(e, axis=1, keepdims=True)
        afft_ref[0, :, r] = aff.T[:N_EXPERTS]


def _post(attn, o_f, o_b, dz, graw, x, gt1, sh2, sc2, dnw, n2w, wau, wdu, wo, wr, tm):
    bsz, s, d = x.shape
    bd = _block_diag(DN_WIDTH, DN_HEAD_DIM, 1.0 / DN_HEAD_DIM)
    dnw_t = jnp.tile(dnw, DN_HEADS).reshape(1, DN_WIDTH)
    wr_p = jnp.zeros((d, LANES), F32).at[:, :N_EXPERTS].set(wr)
    wr_hi = wr_p.astype(BF16)
    wr_lo = (wr_p - wr_hi.astype(F32)).astype(BF16)
    tok = lambda w: pl.BlockSpec((1, tm, w), lambda b, i: (b, i, 0))
    full = lambda t: pl.BlockSpec(t.shape, lambda b, i: (0,) * t.ndim)
    vec = pl.BlockSpec((1, 1, d), lambda b, i: (b, 0, 0))
    sds = jax.ShapeDtypeStruct
    return pl.pallas_call(
        functools.partial(_post_kernel, parts=2 if tm % 256 == 0 else 1),
        grid=(bsz, s // tm),
        in_specs=[tok(ATTN_WIDTH), tok(DN_WIDTH), tok(DN_WIDTH), tok(DN_WIDTH), tok(2 * d), tok(d),
                  vec, vec, vec, full(dnw_t), full(n2w), full(bd), full(wau), full(wdu), full(wo),
                  full(wr_hi), full(wr_lo)],
        out_specs=[tok(d), tok(d), pl.BlockSpec((1, N_EXPERTS, tm), lambda b, i: (b, 0, i))],
        out_shape=[sds((bsz, s, d), F32), sds((bsz, s, d), BF16), sds((bsz, N_EXPERTS, s), F32)],
        compiler_params=_params(("parallel", "parallel")),
        name="post_mixer",
    )(attn, o_f, o_b, dz, graw, x, gt1, sh2, sc2, dnw_t, n2w, bd, wau, wdu, wo, wr_hi, wr_lo)


def _topk_kernel(aff_ref, tri_ref, code_ref, *, cap, s):
    a = aff_ref[0]
    ne = a.shape[0]
    count = lambda m: jnp.sum(jnp.where(m, 1.0, 0.0), axis=1, keepdims=True)

    def vbody(i, lo):
        cand = lo | (jnp.int32(1) << (30 - i))
        return jnp.where(count(a >= pltpu.bitcast(cand, F32)) >= cap, cand, lo)

    lo_bits = lax.fori_loop(0, 31, vbody, jnp.zeros((ne, 1), jnp.int32))
    lo = pltpu.bitcast(lo_bits, F32)
    hi = pltpu.bitcast(lo_bits + 1, F32)

    def rbody(i, lh):
        lo, hi = lh
        mid = 0.5 * (lo + hi)
        ok = count(a >= mid) >= cap
        return jnp.where(ok, mid, lo), jnp.where(ok, hi, mid)

    lo, hi = lax.fori_loop(0, 32, rbody, (lo, hi))
    gt = a >= hi
    eq = (a >= lo) & jnp.logical_not(gt)
    need = cap - count(gt)
    idx = lax.broadcasted_iota(jnp.int32, a.shape, 1)
    nbits = int(np.log2(s))

    def ibody(i, x):
        cand = x | (jnp.int32(1) << (nbits - 1 - i))
        return jnp.where(count(eq & (idx < cand)) < need, cand, x)

    last = lax.fori_loop(0, nbits, ibody, jnp.zeros((ne, 1), jnp.int32))
    sel = gt | (eq & (idx <= last))
    self32 = jnp.where(sel, 1.0, 0.0)
    tri = tri_ref[...]
    carry = jnp.zeros((ne, 1), F32)
    for t in range(s // LANES):
        seg = self32[:, t * LANES:(t + 1) * LANES]
        inc = _dot(seg.astype(BF16), tri)
        pos = (inc - seg + carry).astype(jnp.int32)
        code_ref[0, :, t * LANES:(t + 1) * LANES] = pos * 2 + seg.astype(jnp.int32)
        carry = carry + inc[:, LANES - 1:LANES]


def _topk(afft, cap):
    bsz, ne, s = afft.shape
    i = np.arange(LANES)
    tri = jnp.asarray(np.where(i[:, None] <= i[None, :], 1.0, 0.0), BF16)
    return pl.pallas_call(
        functools.partial(_topk_kernel, cap=cap, s=s),
        grid=(bsz,),
        in_specs=[pl.BlockSpec((1, ne, s), lambda b: (b, 0, 0)),
                  pl.BlockSpec((LANES, LANES), lambda b: (0, 0))],
        out_specs=pl.BlockSpec((1, ne, s), lambda b: (b, 0, 0)),
        out_shape=jax.ShapeDtypeStruct((bsz, ne, s), jnp.int32),
        compiler_params=_params(("parallel",)),
        name="topk",
    )(afft, tri)


def _moe_gather_kernel(starts_ref, ends_ref, code_ref, aff_ref, h2_ref, xe_ref, gate_ref,
                       *, ne, tb, win, fwin, nsub, spt, eg, ns):
    b, g, j = pl.program_id(0), pl.program_id(1), pl.program_id(2)

    @pl.when(j == 0)
    def _():
        xe_ref[...] = jnp.zeros(xe_ref.shape, BF16)
        gate_ref[...] = jnp.zeros(gate_ref.shape, F32)

    experts = [g * eg + i for i in range(eg)]
    for k in range(nsub):
        idx = [(b * ne + experts[i]) * ns + (j * nsub + k) * spt for i in range(eg)]
        a = [pl.multiple_of(starts_ref[idx[i]] // 16 * 16, 16) for i in range(eg)]
        fits = functools.reduce(jnp.logical_and, [ends_ref[idx[i] + spt - 1] - a[i] <= fwin for i in range(eg)])
        cols = slice(k * tb, (k + 1) * tb)

        def emit(n, a=a, cols=cols):
            hot = [code_ref[0, pl.ds(experts[i], 1), cols]
                   == 2 * (lax.broadcasted_iota(jnp.int32, (n, tb), 0) + a[i]) + 1 for i in range(eg)]
            rows = _dot(jnp.concatenate([jnp.where(h, 1.0, 0.0).astype(BF16) for h in hot], axis=0),
                        h2_ref[0, cols, :])
            for i in range(eg):
                sl = pl.ds(a[i], n)
                xe_ref[0, i, sl, :] = xe_ref[0, i, sl, :] + rows[i * n:(i + 1) * n].astype(BF16)
                gsum = jnp.sum(jnp.where(hot[i], aff_ref[0, pl.ds(experts[i], 1), cols], 0.0), axis=1, keepdims=True)
                gate_ref[0, i, sl, :] = gate_ref[0, i, sl, :] + gsum

        pl.when(fits)(functools.partial(emit, fwin))
        pl.when(jnp.logical_not(fits))(functools.partial(emit, win))


def _moe_gather(starts, ends, code, afft, h2, tb, win, fwin, capp, spt):
    bsz, s, d = h2.shape
    ne = code.shape[1]
    eg = 4
    nsub = min(4, s // tb)
    tstep = nsub * tb
    nj = s // tstep
    ns = s // tb * spt
    rows = pl.BlockSpec((1, ne, tstep), lambda b, g, j, st, en: (b, 0, j))
    return pl.pallas_call(
        functools.partial(_moe_gather_kernel, ne=ne, tb=tb, win=win, fwin=fwin, nsub=nsub, spt=spt, eg=eg, ns=ns),
        grid_spec=pltpu.PrefetchScalarGridSpec(
            num_scalar_prefetch=2,
            grid=(bsz, ne // eg, nj),
            in_specs=[rows, rows, pl.BlockSpec((1, tstep, d), lambda b, g, j, st, en: (b, j, 0))],
            out_specs=[pl.BlockSpec((1, eg, capp, d), lambda b, g, j, st, en: (b, g, 0, 0)),
                       pl.BlockSpec((1, eg, capp, 1), lambda b, g, j, st, en: (b, g, 0, 0))]),
        out_shape=[jax.ShapeDtypeStruct((bsz, ne, capp, d), BF16),
                   jax.ShapeDtypeStruct((bsz, ne, capp, 1), F32)],
        compiler_params=_params(("parallel", "parallel", "arbitrary")),
        name="moe_gather",
    )(starts, ends, code, afft, h2)


def _moe_ffn_kernel(xe_ref, gate_ref, wg_ref, wu_ref, wd_ref, y_ref, *, cap):
    xb = xe_ref[0, 0]
    hg = _dot(xb, wg_ref[0, 0].astype(BF16))
    hu = _dot(xb, wu_ref[0, 0].astype(BF16))
    act = (_silu(hg) * hu).astype(BF16)
    y_ref[0, 0, 0:cap, :] = (_dot(act, wd_ref[0, 0].astype(BF16)) * gate_ref[0, 0]).astype(BF16)
    y_ref[0, 0, cap:, :] = jnp.zeros((y_ref.shape[2] - cap, y_ref.shape[3]), BF16)


def _moe_ffn(xe, gates, w_gate, w_up, w_down, cap, capp):
    bsz, ne, _, d = xe.shape
    f = w_gate.shape[-1]
    wspec = lambda r, c: pl.BlockSpec((1, 1, r, c), lambda b, e: (0, e, 0, 0))
    return pl.pallas_call(
        functools.partial(_moe_ffn_kernel, cap=cap),
        grid=(bsz, ne),
        in_specs=[pl.BlockSpec((1, 1, cap, d), lambda b, e: (b, e, 0, 0)),
                  pl.BlockSpec((1, 1, cap, 1), lambda b, e: (b, e, 0, 0)),
                  wspec(d, f), wspec(d, f), wspec(f, d)],
        out_specs=pl.BlockSpec((1, 1, capp, d), lambda b, e: (b, e, 0, 0)),
        out_shape=jax.ShapeDtypeStruct((bsz, ne, capp, d), BF16),
        compiler_params=_params(("parallel", "parallel")),
        name="moe_ffn",
    )(xe, gates, w_gate, w_up, w_down)


TN_DIMS = (((0,), (0,)), ((), ()))


def _moe_scatter_kernel(starts_ref, ends_ref, code_ref, *refs, ne, tb, sub, swin, fwin, ns, wstep):
    y_refs = refs[:ne]
    x1_ref, gt2_ref, o_ref, ycat = refs[ne:]
    b, j = pl.program_id(0), pl.program_id(1)
    spb = tb // sub
    for k in range(spb):
        first = [(b * ne + e) * ns + j * spb for e in range(ne)]
        a_sub = [starts_ref[first[e] + k] // 16 * 16 for e in range(ne)]
        off = [pl.multiple_of(a_sub[e] - starts_ref[first[e]] // wstep * wstep, 16) for e in range(ne)]
        fits = functools.reduce(jnp.logical_and, [ends_ref[first[e] + k] - a_sub[e] <= fwin for e in range(ne)])
        rows = slice(k * sub, (k + 1) * sub)
        hot = lambda e, n: (code_ref[0, e:e + 1, rows] == 2 * (lax.broadcasted_iota(jnp.int32, (n, sub), 0)
                                                               + a_sub[e]) + 1)

        @pl.when(fits)
        def _():
            for e in range(ne):
                ycat[e * fwin:(e + 1) * fwin, :] = y_refs[e][pl.ds(off[e], fwin), :]
            sel = jnp.concatenate([jnp.where(hot(e, fwin), 1.0, 0.0).astype(BF16) for e in range(ne)], axis=0)
            acc = lax.dot_general(sel, ycat[...], TN_DIMS, preferred_element_type=F32)
            o_ref[0, rows, :] = x1_ref[0, rows, :] + gt2_ref[0] * acc

        @pl.when(jnp.logical_not(fits))
        def _():
            acc = None
            for e in range(ne):
                z = lax.dot_general(jnp.where(hot(e, swin), 1.0, 0.0).astype(BF16), y_refs[e][pl.ds(off[e], swin), :],
                                    TN_DIMS, preferred_element_type=F32)
                acc = z if acc is None else acc + z
            o_ref[0, rows, :] = x1_ref[0, rows, :] + gt2_ref[0] * acc


def _moe_scatter(starts, ends, code, y, x1, gt2, tb, win, sub, fwin, wstep):
    bsz, s, d = x1.shape
    ne = code.shape[1]
    nj = s // tb
    ns = s // sub
    spb = tb // sub

    def window(e):
        return pl.BlockSpec((pl.squeezed, pl.squeezed, pl.Element(win), pl.Element(d)),
                            lambda b, j, st, en: (b, e, pl.multiple_of(st[(b * ne + e) * ns + j * spb] // wstep * wstep,
                                                                       wstep), 0))

    return pl.pallas_call(
        functools.partial(_moe_scatter_kernel, ne=ne, tb=tb, sub=sub, swin=sub + 16, fwin=fwin, ns=ns, wstep=wstep),
        grid_spec=pltpu.PrefetchScalarGridSpec(
            num_scalar_prefetch=2,
            grid=(bsz, nj),
            in_specs=[pl.BlockSpec((1, ne, tb), lambda b, j, st, en: (b, 0, j))]
                     + [window(e) for e in range(ne)]
                     + [pl.BlockSpec((1, tb, d), lambda b, j, st, en: (b, j, 0)),
                        pl.BlockSpec((1, 1, d), lambda b, j, st, en: (b, 0, 0))],
            out_specs=pl.BlockSpec((1, tb, d), lambda b, j, st, en: (b, j, 0)),
            scratch_shapes=[pltpu.VMEM((ne * fwin, d), BF16)]),
        out_shape=jax.ShapeDtypeStruct((bsz, s, d), F32),
        compiler_params=_params(("parallel", "parallel")),
        name="moe_scatter",
    )(starts, ends, code, *([y] * ne), x1, gt2)


def _tile(s, pref):
    t = pref
    while s % t:
        t //= 2
    return t


def _tiles(s):
    return {
        "tokens": _tile(s, 512),
        "attn_q": _tile(s, 256),
        "attn_k": _tile(s, 1024),
        "dn_prep": 8 * CHUNK,
        "dn_scan": 4 * CHUNK,
        "moe_block": 256,
        "moe_sub": 128,
    }


def _layer(x, c, w_ada, b_ada, norm1_w, w_in, q_norm_w, k_norm_w, conv_w, a_log, dt_bias, dn_norm_w,
           w_attn_up, w_dn_up, w_o, norm2_w, w_router, w_gate, w_up, w_down):
    bsz, s, d = x.shape
    mod = _ada(c, w_ada, b_ada)
    sh1, sc1, gt1, sh2, sc2, gt2 = [m.reshape(bsz, 1, d) for m in jnp.split(mod, 6, axis=-1)]

    o0 = ATTN_WIDTH + 2 * KV_WIDTH
    o1 = o0 + 3 * DN_WIDTH
    o2 = o1 + DN_WIDTH
    o3 = o2 + 2 * DN_HEADS
    o4 = o3 + 2 * DN_HEADS
    pad = lambda w: jnp.zeros((d, LANES), w.dtype).at[:, :w.shape[1]].set(w)
    ws = [w_in[:, :o0], w_in[:, o0:o1], w_in[:, o1:o2], pad(w_in[:, o2:o3]), pad(w_in[:, o3:o4]), w_in[:, o4:]]
    ws = [w.astype(BF16) for w in ws]
    t = _tiles(s)
    q, kt, v, dqkv, dz, braw, araw, graw = _inproj(x, norm1_w.reshape(1, d), sh1, sc1, ws, q_norm_w, k_norm_w,
                                                   t["tokens"])
    attn = _attention(q, kt, v, t["attn_q"], t["attn_k"])

    prep = _dn_prep(dqkv, braw, araw, conv_w, a_log, dt_bias, t["dn_prep"])
    o_f, o_b = _dn_scan(prep, t["dn_scan"], t["dn_prep"] // CHUNK)

    x1, h2, afft = _post(attn, o_f, o_b, dz, graw, x, gt1, sh2, sc2, dn_norm_w, norm2_w.reshape(1, d),
                         w_attn_up.astype(BF16), w_dn_up.astype(BF16), w_o.astype(BF16), w_router, t["tokens"])

    cap = CAPACITY_FACTOR * s // N_EXPERTS
    tb, sub = t["moe_block"], t["moe_sub"]
    align = 16
    win = tb + align
    code = _topk(afft, cap)
    counts = code[:, :, ::sub] >> 1
    starts = counts.reshape(-1)
    ends = jnp.concatenate([counts[:, :, 1:], jnp.full_like(counts[:, :, :1], cap)], axis=2).reshape(-1)
    rate = N_EXPERTS // CAPACITY_FACTOR
    gwin = 2 * tb // rate + align
    fwin = 2 * sub // rate + align
    wstep = sub
    ywin = wstep + tb + align
    xe, gates = _moe_gather(starts, ends, code, afft, h2, tb, win, gwin, cap + win, tb // sub)
    y = _moe_ffn(xe, gates, w_gate[None], w_up[None], w_down[None], cap, cap + ywin)
    return _moe_scatter(starts, ends, code, y, x1, gt2, tb, ywin, sub, fwin, wstep)


def kernel(x, c, w_ada, b_ada, norm1_w, w_in, q_norm_w, k_norm_w, conv_w, a_log, dt_bias, dn_norm_w,
           w_attn_up, w_dn_up, w_o, norm2_w, w_router, w_gate, w_up, w_down):
    depth = w_ada.shape[0]
    for l in range(depth):
        x = _layer(x, c, w_ada[l], b_ada[l], norm1_w[l], w_in[l], q_norm_w[l], k_norm_w[l], conv_w[l],
                   a_log[l], dt_bias[l], dn_norm_w[l], w_attn_up[l], w_dn_up[l], w_o[l], norm2_w[l],
                   w_router[l], w_gate[l], w_up[l], w_down[l])
    return x
```

```python
import functools

import numpy as np
import jax
import jax.numpy as jnp
from jax import lax
from jax.experimental import pallas as pl
from jax.experimental.pallas import tpu as pltpu

F32 = jnp.float32
BF16 = jnp.bfloat16

GRID_W = 64
N_Q_HEADS = 8
N_KV_HEADS = 2
HEAD_DIM = 64
ATTN_WIDTH = N_Q_HEADS * HEAD_DIM
KV_WIDTH = N_KV_HEADS * HEAD_DIM
ROPE_THETA = 10000.0
DN_HEADS = 8
DN_HEAD_DIM = 64
DN_WIDTH = DN_HEADS * DN_HEAD_DIM
CONV_WIDTH = 5
CHUNK = 64
N_EXPERTS = 16
CAPACITY_FACTOR = 2
EPS = 1e-6
LOG2E = 1.4426950408889634
LANES = 128
VMEM_LIMIT = 56 * 1024 * 1024

NT_DIMS = (((1,), (1,)), ((), ()))


def _params(sem):
    return pltpu.CompilerParams(dimension_semantics=sem, vmem_limit_bytes=VMEM_LIMIT)


def _dot(a, b):
    return jnp.dot(a, b, preferred_element_type=F32)


def _split_dot(x, m, parts):
    acc = None
    r = x
    for i in range(parts):
        h = r.astype(BF16)
        d = _dot(h, m)
        acc = d if acc is None else acc + d
        if i + 1 < parts:
            r = r - h.astype(F32)
    return acc


def _split_dot_l(m, x, parts):
    acc = None
    r = x
    for i in range(parts):
        h = r.astype(BF16)
        d = _dot(m, h)
        acc = d if acc is None else acc + d
        if i + 1 < parts:
            r = r - h.astype(F32)
    return acc


def _silu(x):
    return x * jax.nn.sigmoid(x)


def _ada_kernel(c_ref, w_ref, b_ref, o_ref):
    c = c_ref[...]
    o_ref[...] = jnp.dot(_silu(c), w_ref[...], preferred_element_type=F32,
                         precision=lax.Precision.HIGHEST) + b_ref[...]


def _ada(c, w, b):
    bsz, d = c.shape
    n = w.shape[1]
    tn = 1536
    rows = 8
    c8 = jnp.zeros((rows, d), F32).at[:bsz].set(c)
    out = pl.pallas_call(
        _ada_kernel,
        grid=(n // tn,),
        in_specs=[pl.BlockSpec((rows, d), lambda j: (0, 0)),
                  pl.BlockSpec((d, tn), lambda j: (0, j)),
                  pl.BlockSpec((1, tn), lambda j: (0, j))],
        out_specs=pl.BlockSpec((rows, tn), lambda j: (0, j)),
        out_shape=jax.ShapeDtypeStruct((rows, n), F32),
        compiler_params=_params(("arbitrary",)),
        name="ada",
    )(c8, w, b.reshape(1, n))
    return out[:bsz]


def _norm_rope(x, w, bd, cos, sa, sb):
    n = x.shape[1]
    ms = _dot((x * x).astype(BF16), bd)
    xn = x * lax.rsqrt(ms + EPS) * w
    return xn * cos + pltpu.roll(xn, n - 16, 1) * sa + pltpu.roll(xn, 16, 1) * sb


def _inproj_kernel(x_ref, nw_ref, sh_ref, sc_ref, wa_ref, wd_ref, wz_ref, wb_ref, wal_ref, wg_ref,
                   cos_ref, sa_ref, sb_ref, qw_ref, kw_ref, bdq_ref, bdk_ref,
                   q_ref, kt_ref, v_ref, od_ref, oz_ref, ob_ref, oal_ref, og_ref):
    x = x_ref[0]
    ms = jnp.mean(x * x, axis=-1, keepdims=True)
    h = x * lax.rsqrt(ms + EPS) * nw_ref[...]
    h = h * (1.0 + sc_ref[0]) + sh_ref[0]
    hb = h.astype(BF16)
    a = _dot(hb, wa_ref[...])
    q = a[:, :ATTN_WIDTH]
    k = a[:, ATTN_WIDTH:ATTN_WIDTH + KV_WIDTH]
    v = a[:, ATTN_WIDTH + KV_WIDTH:]
    cos, sa, sb = cos_ref[...], sa_ref[...], sb_ref[...]
    rep = ATTN_WIDTH // LANES
    tile = lambda t: jnp.concatenate([t] * rep, axis=1)
    qr = _norm_rope(q, qw_ref[...], bdq_ref[...], tile(cos), tile(sa), tile(sb))
    q_ref[0] = (qr * (HEAD_DIM ** -0.5 * LOG2E)).astype(BF16)
    kr = _norm_rope(k, kw_ref[...], bdk_ref[...], cos, sa, sb)
    kt = kr.T
    kt_ref[0, 0] = kt[:HEAD_DIM].astype(BF16)
    kt_ref[0, 1] = kt[HEAD_DIM:].astype(BF16)
    lane = lax.broadcasted_iota(jnp.int32, v.shape, 1)
    ones_col = jnp.where(lane == HEAD_DIM, 1.0, 0.0)
    v_ref[0, 0] = jnp.where(lane < HEAD_DIM, v, ones_col).astype(BF16)
    v_ref[0, 1] = jnp.where(lane < HEAD_DIM, pltpu.roll(v, HEAD_DIM, 1), ones_col).astype(BF16)
    od_ref[0] = _dot(hb, wd_ref[...]).astype(BF16)
    oz_ref[0] = _dot(hb, wz_ref[...]).astype(BF16)
    ob_ref[0] = _dot(hb, wb_ref[...])
    oal_ref[0] = _dot(hb, wal_ref[...])
    og_ref[0] = _dot(hb, wg_ref[...]).astype(BF16)


def _rope_tables(s):
    pos = jnp.arange(s)
    lane = jnp.arange(LANES)
    d = lane % HEAD_DIM
    p = d % 32
    f = (p % 16).astype(F32)
    freqs = ROPE_THETA ** (-(f * 2.0 / 32.0))
    axis_pos = jnp.where((d // 32)[None, :] == 0, (pos // GRID_W)[:, None], (pos % GRID_W)[:, None])
    ang = axis_pos.astype(F32) * freqs[None, :]
    cos, sin = jnp.cos(ang), jnp.sin(ang)
    first = (p < 16)[None, :]
    return cos, jnp.where(first, -sin, 0.0), jnp.where(first, 0.0, sin)


def _block_diag(n, blk, val):
    i = np.arange(n)
    return jnp.asarray(np.where((i[:, None] // blk) == (i[None, :] // blk), val, 0.0), BF16)


def _inproj(x, nw, sh, sc, ws, qw, kw, tm):
    bsz, s, d = x.shape
    widths = [w.shape[1] for w in ws[1:]]
    dts = [BF16, BF16, F32, F32, BF16]
    cos, sa, sb = _rope_tables(s)
    bdq = _block_diag(ATTN_WIDTH, HEAD_DIM, 1.0 / HEAD_DIM)
    bdk = _block_diag(KV_WIDTH, HEAD_DIM, 1.0 / HEAD_DIM)
    qw_t = jnp.tile(qw, N_Q_HEADS).reshape(1, ATTN_WIDTH)
    kw_t = jnp.tile(kw, N_KV_HEADS).reshape(1, KV_WIDTH)
    tok = lambda w: pl.BlockSpec((1, tm, w), lambda b, i: (b, i, 0))
    full = lambda a: pl.BlockSpec(a.shape, lambda b, i: (0,) * a.ndim)
    vec = pl.BlockSpec((1, 1, d), lambda b, i: (b, 0, 0))
    tab = pl.BlockSpec((tm, LANES), lambda b, i: (i, 0))
    sds = jax.ShapeDtypeStruct
    return pl.pallas_call(
        _inproj_kernel,
        grid=(bsz, s // tm),
        in_specs=[tok(d), full(nw), vec, vec] + [full(w) for w in ws]
                 + [tab, tab, tab, full(qw_t), full(kw_t), full(bdq), full(bdk)],
        out_specs=[tok(ATTN_WIDTH),
                   pl.BlockSpec((1, N_KV_HEADS, HEAD_DIM, tm), lambda b, i: (b, 0, 0, i)),
                   pl.BlockSpec((1, N_KV_HEADS, tm, LANES), lambda b, i: (b, 0, i, 0))]
                  + [tok(w) for w in widths],
        out_shape=[sds((bsz, s, ATTN_WIDTH), BF16), sds((bsz, N_KV_HEADS, HEAD_DIM, s), BF16),
                   sds((bsz, N_KV_HEADS, s, LANES), BF16)]
                  + [sds((bsz, s, w), dt) for w, dt in zip(widths, dts)],
        compiler_params=_params(("parallel", "parallel")),
        name="inproj",
    )(x, nw, sh, sc, *ws, cos, sa, sb, qw_t, kw_t, bdq, bdk)


def _attn_kernel(q_ref, kt_ref, v_ref, o_ref, *, tq, tk, group):
    rows = group * tq
    q = q_ref[0]
    qs = jnp.concatenate([q[:, h * HEAD_DIM:(h + 1) * HEAD_DIM] for h in range(group)], axis=0)
    nk = kt_ref.shape[3] // tk
    scores = lambda k: _dot(qs, kt_ref[0, 0, :, k * tk:(k + 1) * tk])
    m = jnp.full((rows, LANES), -jnp.inf, F32)
    acc = jnp.zeros((rows, LANES), F32)
    s_next = scores(0)
    for k in range(nk):
        s = s_next
        if k + 1 < nk:
            s_next = scores(k + 1)
        m_new = jnp.maximum(m, jnp.max(s, axis=1, keepdims=True))
        alpha = jnp.exp2(m - m_new)
        p = jnp.concatenate([jnp.exp2(s[:, t * LANES:(t + 1) * LANES] - m_new).astype(BF16)
                             for t in range(tk // LANES)], axis=1)
        acc = alpha * acc + _dot(p, v_ref[0, 0, k * tk:(k + 1) * tk, :])
        m = m_new
    o = acc[:, :HEAD_DIM] * (1.0 / acc[:, HEAD_DIM:HEAD_DIM + 1])
    o_ref[0] = jnp.concatenate([o[h * tq:(h + 1) * tq] for h in range(group)], axis=1).astype(BF16)


def _attention(q, kt, v, tq, tk):
    bsz, s, _ = q.shape
    group = N_Q_HEADS // N_KV_HEADS
    gw = group * HEAD_DIM
    return pl.pallas_call(
        functools.partial(_attn_kernel, tq=tq, tk=tk, group=group),
        grid=(bsz, N_KV_HEADS, s // tq),
        in_specs=[pl.BlockSpec((1, tq, gw), lambda b, g, i: (b, i, g)),
                  pl.BlockSpec((1, 1, HEAD_DIM, s), lambda b, g, i: (b, g, 0, 0)),
                  pl.BlockSpec((1, 1, s, LANES), lambda b, g, i: (b, g, 0, 0))],
        out_specs=pl.BlockSpec((1, tq, gw), lambda b, g, i: (b, i, g)),
        out_shape=jax.ShapeDtypeStruct((bsz, s, ATTN_WIDTH), BF16),
        compiler_params=_params(("parallel", "parallel", "parallel")),
        name="attention",
    )(q, kt, v)


def _dn_prep_kernel(cur_ref, prev_ref, next_ref, cw_ref, braw_ref, araw_ref, alog_ref, dtb_ref,
                    bd_ref, trif_ref, trib_ref, sel_ref, exf_ref, exb_ref,
                    kn_ref, qn_ref, vb_ref, win_ref, qg_ref, kdt_ref, gcx_ref, betax_ref, gl_ref,
                    *, ts):
    i = pl.program_id(1)
    halo = prev_ref.shape[1]
    ext = jnp.concatenate([jnp.where(i > 0, prev_ref[0].astype(F32), 0.0),
                           cur_ref[0].astype(F32),
                           jnp.where(i < pl.num_programs(1) - 1, next_ref[0].astype(F32), 0.0)], axis=0)
    cw = cw_ref[...]
    conv = None
    for j in range(CONV_WIDTH):
        shift = (CONV_WIDTH // 2 - j) % ext.shape[0]
        tap = (pltpu.roll(ext, shift, 0) if shift else ext)[halo:halo + ts] * cw[j:j + 1, :]
        conv = tap if conv is None else conv + tap
    act = _silu(conv)
    cq, ck, cv = act[:, :DN_WIDTH], act[:, DN_WIDTH:2 * DN_WIDTH], act[:, 2 * DN_WIDTH:]
    bd = bd_ref[...]
    qn = cq * lax.rsqrt(_dot((cq * cq).astype(BF16), bd) + EPS) * (DN_HEAD_DIM ** -0.5)
    kn = ck * lax.rsqrt(_dot((ck * ck).astype(BF16), bd) + EPS)
    kn_ref[0] = kn.astype(BF16)
    qn_ref[0] = qn.astype(BF16)

    beta = jax.nn.sigmoid(braw_ref[0])
    g = -jnp.exp(alog_ref[...]) * jax.nn.softplus(araw_ref[0] + dtb_ref[...])
    gl = jnp.exp(_split_dot_l(sel_ref[...], g, 3))
    gcs = [_split_dot_l(trif_ref[...], g, 3), _split_dot_l(trib_ref[...], g, 3)]
    for d, ex_ref in enumerate((exf_ref, exb_ref)):
        gc = gcs[d]
        ex = ex_ref[...]
        beta_x = _dot(beta.astype(BF16), ex)
        gc_x = _split_dot(gc, ex, 3)
        ek_x = _dot(jnp.exp(gcs[1 - d] - g).astype(BF16), ex)
        eg_x = jnp.exp(gc_x)
        vb_ref[d, 0] = (cv * beta_x).astype(BF16)
        win_ref[d, 0] = (kn * beta_x * eg_x).astype(BF16)
        qg_ref[d, 0] = (qn * eg_x).astype(BF16)
        kdt_ref[d, 0] = (kn * ek_x).T.astype(BF16)
        gcx_ref[d, 0] = gc_x
        betax_ref[d, 0] = beta_x
        gl_ref[d, 0] = _split_dot(gl, ex, 2)


def _dn_prep(dqkv, braw, araw, conv_w, a_log, dt_bias, ts):
    bsz, s, w3 = dqkv.shape
    nt = s // ts
    cpt = ts // CHUNK
    assert cpt == 8
    halo = 16
    hb = ts // halo
    idx = np.arange(ts)
    same = (idx[:, None] // CHUNK) == (idx[None, :] // CHUNK)
    trif = jnp.asarray(np.where(same & (idx[:, None] >= idx[None, :]), 1.0, 0.0), BF16)
    trib = jnp.asarray(np.where(same & (idx[:, None] <= idx[None, :]), 1.0, 0.0), BF16)
    sel = jnp.asarray(np.where(np.arange(cpt)[:, None] == (idx[None, :] // CHUNK), 1.0, 0.0), BF16)
    bd = _block_diag(DN_WIDTH, DN_HEAD_DIM, 1.0)
    lane = np.arange(DN_WIDTH) // DN_HEAD_DIM
    row = np.arange(LANES)
    exf = jnp.asarray(np.where(row[:, None] == lane[None, :], 1.0, 0.0), BF16)
    exb = jnp.asarray(np.where(row[:, None] == lane[None, :] + DN_HEADS, 1.0, 0.0), BF16)
    nh2 = 2 * DN_HEADS
    alog = jnp.zeros((1, LANES), F32).at[0, :nh2].set(a_log.reshape(nh2))
    dtb = jnp.zeros((1, LANES), F32).at[0, :nh2].set(dt_bias.reshape(nh2))
    full = lambda t: pl.BlockSpec(t.shape, lambda b, i: (0,) * t.ndim)
    tok = lambda w: pl.BlockSpec((1, ts, w), lambda b, i: (b, i, 0))
    dtok = lambda w: pl.BlockSpec((2, 1, ts, w), lambda b, i: (0, b, i, 0))
    sds = jax.ShapeDtypeStruct
    return pl.pallas_call(
        functools.partial(_dn_prep_kernel, ts=ts),
        grid=(bsz, nt),
        in_specs=[tok(w3),
                  pl.BlockSpec((1, halo, w3), lambda b, i: (b, jnp.maximum(i * hb - 1, 0), 0)),
                  pl.BlockSpec((1, halo, w3), lambda b, i: (b, jnp.minimum((i + 1) * hb, s // halo - 1), 0)),
                  full(conv_w), tok(LANES), tok(LANES), full(alog), full(dtb),
                  full(bd), full(trif), full(trib), full(sel), full(exf), full(exb)],
        out_specs=[tok(DN_WIDTH), tok(DN_WIDTH), dtok(DN_WIDTH), dtok(DN_WIDTH), dtok(DN_WIDTH),
                   pl.BlockSpec((2, 1, DN_WIDTH, ts), lambda b, i: (0, b, 0, i)),
                   dtok(DN_WIDTH), dtok(DN_WIDTH),
                   pl.BlockSpec((2, 1, cpt, DN_WIDTH), lambda b, i: (0, b, i, 0))],
        out_shape=[sds((bsz, s, DN_WIDTH), BF16), sds((bsz, s, DN_WIDTH), BF16),
                   sds((2, bsz, s, DN_WIDTH), BF16), sds((2, bsz, s, DN_WIDTH), BF16),
                   sds((2, bsz, s, DN_WIDTH), BF16), sds((2, bsz, DN_WIDTH, s), BF16),
                   sds((2, bsz, s, DN_WIDTH), F32), sds((2, bsz, s, DN_WIDTH), F32),
                   sds((2, bsz, s // CHUNK, DN_WIDTH), F32)],
        compiler_params=_params(("parallel", "parallel")),
        name="dn_prep",
    )(dqkv, dqkv, dqkv, conv_w, braw, araw, alog, dtb, bd, trif, trib, sel, exf, exb)


def _pair_diag(x):
    left = lax.broadcasted_iota(jnp.int32, x.shape, 1) < DN_HEAD_DIM
    zero = jnp.zeros_like(x)
    return jnp.concatenate([jnp.where(left, x, zero), jnp.where(left, zero, x)], axis=0)


def _dn_scan_kernel(*refs, cpb, gl_rows):
    nin = 9
    ins = (refs[:nin], refs[nin:2 * nin])
    outs = refs[2 * nin:2 * nin + 2]
    state = refs[2 * nin + 2]
    j = pl.program_id(1)
    nb = pl.num_programs(1)

    @pl.when(j == 0)
    def _():
        state[...] = jnp.zeros(state.shape, F32)

    ri = lax.broadcasted_iota(jnp.int32, (CHUNK, LANES), 0)
    ci = lax.broadcasted_iota(jnp.int32, (CHUNK, LANES), 1) % CHUNK
    incl = (ri >= ci, ri <= ci)
    strict = (ri > ci, ri < ci)
    diag = ri == ci
    eye = jnp.where(diag, 1.0, 0.0)
    merge = [((ri // (2 * sz)) == (ci // (2 * sz))) & ((ri // sz) != (ci // sz))
             for sz in (1, 2, 4, 8, 16, 32)]
    rd = lax.broadcasted_iota(jnp.int32, (LANES, LANES), 0) // DN_HEAD_DIM
    cd = lax.broadcasted_iota(jnp.int32, (LANES, LANES), 1) // DN_HEAD_DIM
    same_head = rd == cd

    dirs = (0, 1)
    pairs = range(DN_HEADS // 2)
    items = [(d, c, p) for d in dirs for c in range(cpb) for p in pairs]
    rows = [slice(c * CHUNK, (c + 1) * CHUNK) for c in range(cpb)]
    ls = [slice(p * LANES, (p + 1) * LANES) for p in pairs]
    kq = {(d, c, p): jnp.concatenate([ins[d][0][0, rows[c], ls[p]], ins[d][1][0, rows[c], ls[p]]], axis=0)
          for d, c, p in items}
    a = {i: lax.dot_general(kq[i], _pair_diag(kq[i][:CHUNK]), NT_DIMS, preferred_element_type=F32)
         for i in items}
    dec, lm, t = {}, {}, {}
    for d, c, p in items:
        gcx = ins[d][6][0, 0, rows[c], ls[p]]
        grow = jnp.sum(jnp.where(diag, gcx, 0.0), axis=0, keepdims=True)
        i = (d, c, p)
        dec[i] = jnp.exp(jnp.where(incl[d], gcx - grow, -jnp.inf))
        lm[i] = jnp.where(strict[d], a[i][:CHUNK] * ins[d][7][0, 0, rows[c], ls[p]] * dec[i], 0.0)
        t[i] = eye - jnp.where(merge[0], lm[i], 0.0)
    for mk in merge[1:]:
        x = {i: _dot(jnp.where(mk, lm[i], 0.0).astype(BF16), _pair_diag(t[i].astype(BF16))).astype(BF16)
             for i in items}
        t = {i: t[i] - _dot(t[i].astype(BF16), _pair_diag(x[i])) for i in items}
    uw = {(d, c, p): _dot(t[d, c, p].astype(BF16),
                          jnp.concatenate([_pair_diag(ins[d][2][0, 0, rows[c], ls[p]]),
                                           _pair_diag(ins[d][3][0, 0, rows[c], ls[p]])], axis=1))
          for d, c, p in items}
    a_in = {i: (a[i][CHUNK:] * dec[i]).astype(BF16) for i in items}

    zeros = jnp.zeros((CHUNK, LANES), BF16)
    blk = (j, nb - 1 - j)
    for step in range(cpb):
        chunk = (step, cpb - 1 - step)
        live = [(d, chunk[d], p) for d in dirs for p in pairs]
        gl_c = [ins[d][8][0, 0, pl.ds((blk[d] * cpb) % gl_rows + chunk[d], 1), :] for d in dirs]
        s_old = {(d, p): state[d, p] for d in dirs for p in pairs}
        wq = {(d, c, p): _dot(jnp.concatenate([uw[d, c, p][:, LANES:].astype(BF16),
                                               ins[d][4][0, 0, rows[c], ls[p]]], axis=0),
                              s_old[d, p].astype(BF16)) for d, c, p in live}
        vnb = {i: (uw[i][:, :LANES] - wq[i][:CHUNK]).astype(BF16) for i in live}
        o_in = {i: _dot(a_in[i], _pair_diag(vnb[i])) for i in live}
        ds = {(d, c, p): _dot(ins[d][5][0, 0, ls[p], (c // 2) * LANES:(c // 2 + 1) * LANES],
                              jnp.concatenate([vnb[d, c, p], zeros] if c % 2 == 0 else [zeros, vnb[d, c, p]], axis=0))
              for d, c, p in live}
        for d, c, p in live:
            state[d, p] = s_old[d, p] * gl_c[d][:, ls[p]] + jnp.where(same_head, ds[d, c, p], 0.0)
            outs[d][0, rows[c], ls[p]] = wq[d, c, p][CHUNK:] + o_in[d, c, p]


def _dn_scan(prep, tsb, gl_rows):
    kn, qn, vb, win, qg, kdt, gcx, betax, gl = prep
    bsz, s, _ = kn.shape
    nb = s // tsb
    cpb = tsb // CHUNK
    assert cpb % 2 == 0
    toks, ins = [], []
    for d in (0, 1):
        blk = (lambda j: j) if d == 0 else (lambda j: nb - 1 - j)
        tok = pl.BlockSpec((1, tsb, DN_WIDTH), lambda b, j, blk=blk: (b, blk(j), 0))
        dtok = pl.BlockSpec((1, 1, tsb, DN_WIDTH), lambda b, j, blk=blk, d=d: (d, b, blk(j), 0))
        toks.append(tok)
        ins += [tok, tok, dtok, dtok, dtok,
                pl.BlockSpec((1, 1, DN_WIDTH, tsb), lambda b, j, blk=blk, d=d: (d, b, 0, blk(j))),
                dtok, dtok,
                pl.BlockSpec((1, 1, gl_rows, DN_WIDTH), lambda b, j, blk=blk, d=d: (d, b, (blk(j) * cpb) // gl_rows, 0))]
    args = (kn, qn, vb, win, qg, kdt, gcx, betax, gl)
    out = jax.ShapeDtypeStruct((bsz, s, DN_WIDTH), F32)
    return pl.pallas_call(
        functools.partial(_dn_scan_kernel, cpb=cpb, gl_rows=gl_rows),
        grid=(bsz, nb),
        in_specs=ins,
        out_specs=toks,
        out_shape=[out, out],
        scratch_shapes=[pltpu.VMEM((2, DN_HEADS // 2, LANES, LANES), F32)],
        compiler_params=_params(("parallel", "arbitrary")),
        name="dn_scan",
    )(*args, *args)


def _post_kernel(attn_ref, of_ref, ob_ref, dz_ref, g_ref, x_ref, gt1_ref, sh2_ref, sc2_ref,
                 dnw_ref, n2w_ref, bd_ref, wau_ref, wdu_ref, wo_ref, wrh_ref, wrl_ref,
                 x1_ref, h2_ref, afft_ref, *, parts):
    tm, d = x_ref.shape[1], x_ref.shape[2]
    rows = [slice(i * (tm // parts), (i + 1) * (tm // parts)) for i in range(parts)]
    o = [of_ref[0, r, :] + ob_ref[0, r, :] for r in rows]
    ms = [_dot((v * v).astype(BF16), bd_ref[...]) for v in o]
    dn = [(v * lax.rsqrt(m + EPS) * dnw_ref[...] * _silu(dz_ref[0, r, :].astype(F32))).astype(BF16)
          for v, m, r in zip(o, ms, rows)]
    up_a = [_dot(attn_ref[0, r, :], wau_ref[...]) for r in rows]
    up_d = [_dot(v, wdu_ref[...]) for v in dn]
    merged = [(jax.nn.sigmoid(g_ref[0, r, :d].astype(F32)) * a
               + jax.nn.sigmoid(g_ref[0, r, d:].astype(F32)) * b).astype(BF16)
              for r, a, b in zip(rows, up_a, up_d)]
    mixed = [_dot(v, wo_ref[...]) for v in merged]
    h2s = []
    for r, v in zip(rows, mixed):
        x1 = x_ref[0, r, :] + gt1_ref[0] * v
        x1_ref[0, r, :] = x1
        ms2 = jnp.mean(x1 * x1, axis=-1, keepdims=True)
        h2 = x1 * lax.rsqrt(ms2 + EPS) * n2w_ref[...]
        h2 = h2 * (1.0 + sc2_ref[0]) + sh2_ref[0]
        h2_ref[0, r, :] = h2.astype(BF16)
        h2s.append(h2)
    his = [v.astype(BF16) for v in h2s]
    los = [(v - h.astype(F32)).astype(BF16) for v, h in zip(h2s, his)]
    logits = [_dot(h, wrh_ref[...]) + _dot(l, wrh_ref[...]) + _dot(h, wrl_ref[...]) for h, l in zip(his, los)]
    for r, lg in zip(rows, logits):
        lane = lax.broadcasted_iota(jnp.int32, lg.shape, 1)
        lg = jnp.where(lane < N_EXPERTS, lg, -jnp.inf)
        e = jnp.exp(lg - jnp.max(lg, axis=1, keepdims=True))
        aff = e / jnp.sum(e, axis=1, keepdims=True)
        afft_ref[0, :, r] = aff.T[:N_EXPERTS]


def _post(attn, o_f, o_b, dz, graw, x, gt1, sh2, sc2, dnw, n2w, wau, wdu, wo, wr, tm):
    bsz, s, d = x.shape
    bd = _block_diag(DN_WIDTH, DN_HEAD_DIM, 1.0 / DN_HEAD_DIM)
    dnw_t = jnp.tile(dnw, DN_HEADS).reshape(1, DN_WIDTH)
    wr_p = jnp.zeros((d, LANES), F32).at[:, :N_EXPERTS].set(wr)
    wr_hi = wr_p.astype(BF16)
    wr_lo = (wr_p - wr_hi.astype(F32)).astype(BF16)
    tok = lambda w: pl.BlockSpec((1, tm, w), lambda b, i: (b, i, 0))
    full = lambda t: pl.BlockSpec(t.shape, lambda b, i: (0,) * t.ndim)
    vec = pl.BlockSpec((1, 1, d), lambda b, i: (b, 0, 0))
    sds = jax.ShapeDtypeStruct
    return pl.pallas_call(
        functools.partial(_post_kernel, parts=2 if tm % 256 == 0 else 1),
        grid=(bsz, s // tm),
        in_specs=[tok(ATTN_WIDTH), tok(DN_WIDTH), tok(DN_WIDTH), tok(DN_WIDTH), tok(2 * d), tok(d),
                  vec, vec, vec, full(dnw_t), full(n2w), full(bd), full(wau), full(wdu), full(wo),
                  full(wr_hi), full(wr_lo)],
        out_specs=[tok(d), tok(d), pl.BlockSpec((1, N_EXPERTS, tm), lambda b, i: (b, 0, i))],
        out_shape=[sds((bsz, s, d), F32), sds((bsz, s, d), BF16), sds((bsz, N_EXPERTS, s), F32)],
        compiler_params=_params(("parallel", "parallel")),
        name="post_mixer",
    )(attn, o_f, o_b, dz, graw, x, gt1, sh2, sc2, dnw_t, n2w, bd, wau, wdu, wo, wr_hi, wr_lo)


def _topk_kernel(aff_ref, tri_ref, code_ref, *, cap, s):
    a = aff_ref[0]
    ne = a.shape[0]
    count = lambda m: jnp.sum(jnp.where(m, 1.0, 0.0), axis=1, keepdims=True)

    def vbody(i, lo):
        cand = lo | (jnp.int32(1) << (30 - i))
        return jnp.where(count(a >= pltpu.bitcast(cand, F32)) >= cap, cand, lo)

    lo_bits = lax.fori_loop(0, 31, vbody, jnp.zeros((ne, 1), jnp.int32))
    lo = pltpu.bitcast(lo_bits, F32)
    hi = pltpu.bitcast(lo_bits + 1, F32)

    def rbody(i, lh):
        lo, hi = lh
        mid = 0.5 * (lo + hi)
        ok = count(a >= mid) >= cap
        return jnp.where(ok, mid, lo), jnp.where(ok, hi, mid)

    lo, hi = lax.fori_loop(0, 32, rbody, (lo, hi))
    gt = a >= hi
    eq = (a >= lo) & jnp.logical_not(gt)
    need = cap - count(gt)
    idx = lax.broadcasted_iota(jnp.int32, a.shape, 1)
    nbits = int(np.log2(s))

    def ibody(i, x):
        cand = x | (jnp.int32(1) << (nbits - 1 - i))
        return jnp.where(count(eq & (idx < cand)) < need, cand, x)

    last = lax.fori_loop(0, nbits, ibody, jnp.zeros((ne, 1), jnp.int32))
    sel = gt | (eq & (idx <= last))
    self32 = jnp.where(sel, 1.0, 0.0)
    tri = tri_ref[...]
    carry = jnp.zeros((ne, 1), F32)
    for t in range(s // LANES):
        seg = self32[:, t * LANES:(t + 1) * LANES]
        inc = _dot(seg.astype(BF16), tri)
        pos = (inc - seg + carry).astype(jnp.int32)
        code_ref[0, :, t * LANES:(t + 1) * LANES] = pos * 2 + seg.astype(jnp.int32)
        carry = carry + inc[:, LANES - 1:LANES]


def _topk(afft, cap):
    bsz, ne, s = afft.shape
    i = np.arange(LANES)
    tri = jnp.asarray(np.where(i[:, None] <= i[None, :], 1.0, 0.0), BF16)
    code = pl.pallas_call(
        functools.partial(_topk_kernel, cap=cap, s=s),
        grid=(1,),
        in_specs=[pl.BlockSpec((1, bsz * ne, s), lambda b: (0, 0, 0)),
                  pl.BlockSpec((LANES, LANES), lambda b: (0, 0))],
        out_specs=pl.BlockSpec((1, bsz * ne, s), lambda b: (0, 0, 0)),
        out_shape=jax.ShapeDtypeStruct((1, bsz * ne, s), jnp.int32),
        compiler_params=_params(("arbitrary",)),
        name="topk",
    )(afft.reshape(1, bsz * ne, s), tri)
    return code.reshape(bsz, ne, s)


def _moe_gather_kernel(starts_ref, ends_ref, code_ref, aff_ref, h2_ref, xe_ref, gate_ref,
                       *, ne, tb, win, fwin, nsub, spt, eg, ns):
    b, g, j = pl.program_id(0), pl.program_id(1), pl.program_id(2)

    @pl.when(j == 0)
    def _():
        xe_ref[...] = jnp.zeros(xe_ref.shape, BF16)
        gate_ref[...] = jnp.zeros(gate_ref.shape, F32)

    experts = [g * eg + i for i in range(eg)]
    for k in range(nsub):
        idx = [(b * ne + experts[i]) * ns + (j * nsub + k) * spt for i in range(eg)]
        a = [pl.multiple_of(starts_ref[idx[i]] // 16 * 16, 16) for i in range(eg)]
        fits = functools.reduce(jnp.logical_and, [ends_ref[idx[i] + spt - 1] - a[i] <= fwin for i in range(eg)])
        cols = slice(k * tb, (k + 1) * tb)

        def emit(n, a=a, cols=cols):
            hot = [code_ref[0, pl.ds(experts[i], 1), cols]
                   == 2 * (lax.broadcasted_iota(jnp.int32, (n, tb), 0) + a[i]) + 1 for i in range(eg)]
            rows = _dot(jnp.concatenate([jnp.where(h, 1.0, 0.0).astype(BF16) for h in hot], axis=0),
                        h2_ref[0, cols, :])
            for i in range(eg):
                sl = pl.ds(a[i], n)
                xe_ref[0, i, sl, :] = xe_ref[0, i, sl, :] + rows[i * n:(i + 1) * n].astype(BF16)
                gsum = jnp.sum(jnp.where(hot[i], aff_ref[0, pl.ds(experts[i], 1), cols], 0.0), axis=1, keepdims=True)
                gate_ref[0, i, sl, :] = gate_ref[0, i, sl, :] + gsum

        pl.when(fits)(functools.partial(emit, fwin))
        pl.when(jnp.logical_not(fits))(functools.partial(emit, win))


def _moe_gather(starts, ends, code, afft, h2, tb, win, fwin, capp, spt):
    bsz, s, d = h2.shape
    ne = code.shape[1]
    eg = 4
    nsub = min(4, s // tb)
    tstep = nsub * tb
    nj = s // tstep
    ns = s // tb * spt
    rows = pl.BlockSpec((1, ne, tstep), lambda b, g, j, st, en: (b, 0, j))
    return pl.pallas_call(
        functools.partial(_moe_gather_kernel, ne=ne, tb=tb, win=win, fwin=fwin, nsub=nsub, spt=spt, eg=eg, ns=ns),
        grid_spec=pltpu.PrefetchScalarGridSpec(
            num_scalar_prefetch=2,
            grid=(bsz, ne // eg, nj),
            in_specs=[rows, rows, pl.BlockSpec((1, tstep, d), lambda b, g, j, st, en: (b, j, 0))],
            out_specs=[pl.BlockSpec((1, eg, capp, d), lambda b, g, j, st, en: (b, g, 0, 0)),
                       pl.BlockSpec((1, eg, capp, 1), lambda b, g, j, st, en: (b, g, 0, 0))]),
        out_shape=[jax.ShapeDtypeStruct((bsz, ne, capp, d), BF16),
                   jax.ShapeDtypeStruct((bsz, ne, capp, 1), F32)],
        compiler_params=_params(("parallel", "parallel", "arbitrary")),
        name="moe_gather",
    )(starts, ends, code, afft, h2)


def _moe_ffn_kernel(xe_ref, gate_ref, wg_ref, wu_ref, wd_ref, y_ref, *, cap):
    xb = xe_ref[0, 0]
    hg = _dot(xb, wg_ref[0, 0].astype(BF16))
    hu = _dot(xb, wu_ref[0, 0].astype(BF16))
    act = (_silu(hg) * hu).astype(BF16)
    y_ref[0, 0, 0:cap, :] = (_dot(act, wd_ref[0, 0].astype(BF16)) * gate_ref[0, 0]).astype(BF16)
    y_ref[0, 0, cap:, :] = jnp.zeros((y_ref.shape[2] - cap, y_ref.shape[3]), BF16)


def _moe_ffn(xe, gates, w_gate, w_up, w_down, cap, capp):
    bsz, ne, _, d = xe.shape
    f = w_gate.shape[-1]
    wspec = lambda r, c: pl.BlockSpec((1, 1, r, c), lambda b, e: (0, e, 0, 0))
    return pl.pallas_call(
        functools.partial(_moe_ffn_kernel, cap=cap),
        grid=(bsz, ne),
        in_specs=[pl.BlockSpec((1, 1, cap, d), lambda b, e: (b, e, 0, 0)),
                  pl.BlockSpec((1, 1, cap, 1), lambda b, e: (b, e, 0, 0)),
                  wspec(d, f), wspec(d, f), wspec(f, d)],
        out_specs=pl.BlockSpec((1, 1, capp, d), lambda b, e: (b, e, 0, 0)),
        out_shape=jax.ShapeDtypeStruct((bsz, ne, capp, d), BF16),
        compiler_params=_params(("parallel", "parallel")),
        name="moe_ffn",
    )(xe, gates, w_gate, w_up, w_down)


TN_DIMS = (((0,), (0,)), ((), ()))


def _moe_scatter_kernel(starts_ref, ends_ref, code_ref, *refs, ne, tb, sub, swin, fwin, ns, wstep):
    y_refs = refs[:ne]
    x1_ref, gt2_ref, o_ref, ycat = refs[ne:]
    b, j = pl.program_id(0), pl.program_id(1)
    spb = tb // sub
    for k in range(spb):
        first = [(b * ne + e) * ns + j * spb for e in range(ne)]
        a_sub = [starts_ref[first[e] + k] // 16 * 16 for e in range(ne)]
        off = [pl.multiple_of(a_sub[e] - starts_ref[first[e]] // wstep * wstep, 16) for e in range(ne)]
        fits = functools.reduce(jnp.logical_and, [ends_ref[first[e] + k] - a_sub[e] <= fwin for e in range(ne)])
        rows = slice(k * sub, (k + 1) * sub)
        hot = lambda e, n: (code_ref[0, e:e + 1, rows] == 2 * (lax.broadcasted_iota(jnp.int32, (n, sub), 0)
                                                               + a_sub[e]) + 1)

        @pl.when(fits)
        def _():
            for e in range(ne):
                ycat[e * fwin:(e + 1) * fwin, :] = y_refs[e][pl.ds(off[e], fwin), :]
            sel = jnp.concatenate([jnp.where(hot(e, fwin), 1.0, 0.0).astype(BF16) for e in range(ne)], axis=0)
            acc = lax.dot_general(sel, ycat[...], TN_DIMS, preferred_element_type=F32)
            o_ref[0, rows, :] = x1_ref[0, rows, :] + gt2_ref[0] * acc

        @pl.when(jnp.logical_not(fits))
        def _():
            acc = None
            for e in range(ne):
                z = lax.dot_general(jnp.where(hot(e, swin), 1.0, 0.0).astype(BF16), y_refs[e][pl.ds(off[e], swin), :],
                                    TN_DIMS, preferred_element_type=F32)
                acc = z if acc is None else acc + z
            o_ref[0, rows, :] = x1_ref[0, rows, :] + gt2_ref[0] * acc


def _moe_scatter(starts, ends, code, y, x1, gt2, tb, win, sub, fwin, wstep):
    bsz, s, d = x1.shape
    ne = code.shape[1]
    nj = s // tb
    ns = s // sub
    spb = tb // sub

    def window(e):
        return pl.BlockSpec((pl.squeezed, pl.squeezed, pl.Element(win), pl.Element(d)),
                            lambda b, j, st, en: (b, e, pl.multiple_of(st[(b * ne + e) * ns + j * spb] // wstep * wstep,
                                                                       wstep), 0))

    return pl.pallas_call(
        functools.partial(_moe_scatter_kernel, ne=ne, tb=tb, sub=sub, swin=sub + 16, fwin=fwin, ns=ns, wstep=wstep),
        grid_spec=pltpu.PrefetchScalarGridSpec(
            num_scalar_prefetch=2,
            grid=(bsz, nj),
            in_specs=[pl.BlockSpec((1, ne, tb), lambda b, j, st, en: (b, 0, j))]
                     + [window(e) for e in range(ne)]
                     + [pl.BlockSpec((1, tb, d), lambda b, j, st, en: (b, j, 0)),
                        pl.BlockSpec((1, 1, d), lambda b, j, st, en: (b, 0, 0))],
            out_specs=pl.BlockSpec((1, tb, d), lambda b, j, st, en: (b, j, 0)),
            scratch_shapes=[pltpu.VMEM((ne * fwin, d), BF16)]),
        out_shape=jax.ShapeDtypeStruct((bsz, s, d), F32),
        compiler_params=_params(("parallel", "parallel")),
        name="moe_scatter",
    )(starts, ends, code, *([y] * ne), x1, gt2)


def _tile(s, pref):
    t = pref
    while s % t:
        t //= 2
    return t


def _tiles(s):
    return {
        "tokens": _tile(s, 512),
        "attn_q": _tile(s, 256),
        "attn_k": _tile(s, 1024),
        "dn_prep": 8 * CHUNK,
        "dn_scan": 4 * CHUNK,
        "moe_block": 256,
        "moe_sub": 128,
    }


def _layer(x, c, w_ada, b_ada, norm1_w, w_in, q_norm_w, k_norm_w, conv_w, a_log, dt_bias, dn_norm_w,
           w_attn_up, w_dn_up, w_o, norm2_w, w_router, w_gate, w_up, w_down):
    bsz, s, d = x.shape
    mod = _ada(c, w_ada, b_ada)
    sh1, sc1, gt1, sh2, sc2, gt2 = [m.reshape(bsz, 1, d) for m in jnp.split(mod, 6, axis=-1)]

    o0 = ATTN_WIDTH + 2 * KV_WIDTH
    o1 = o0 + 3 * DN_WIDTH
    o2 = o1 + DN_WIDTH
    o3 = o2 + 2 * DN_HEADS
    o4 = o3 + 2 * DN_HEADS
    pad = lambda w: jnp.zeros((d, LANES), w.dtype).at[:, :w.shape[1]].set(w)
    ws = [w_in[:, :o0], w_in[:, o0:o1], w_in[:, o1:o2], pad(w_in[:, o2:o3]), pad(w_in[:, o3:o4]), w_in[:, o4:]]
    ws = [w.astype(BF16) for w in ws]
    t = _tiles(s)
    q, kt, v, dqkv, dz, braw, araw, graw = _inproj(x, norm1_w.reshape(1, d), sh1, sc1, ws, q_norm_w, k_norm_w,
                                                   t["tokens"])
    attn = _attention(q, kt, v, t["attn_q"], t["attn_k"])

    prep = _dn_prep(dqkv, braw, araw, conv_w, a_log, dt_bias, t["dn_prep"])
    o_f, o_b = _dn_scan(prep, t["dn_scan"], t["dn_prep"] // CHUNK)

    x1, h2, afft = _post(attn, o_f, o_b, dz, graw, x, gt1, sh2, sc2, dn_norm_w, norm2_w.reshape(1, d),
                         w_attn_up.astype(BF16), w_dn_up.astype(BF16), w_o.astype(BF16), w_router, t["tokens"])

    cap = CAPACITY_FACTOR * s // N_EXPERTS
    tb, sub = t["moe_block"], t["moe_sub"]
    align = 16
    win = tb + align
    code = _topk(afft, cap)
    counts = code[:, :, ::sub] >> 1
    starts = counts.reshape(-1)
    ends = jnp.concatenate([counts[:, :, 1:], jnp.full_like(counts[:, :, :1], cap)], axis=2).reshape(-1)
    rate = N_EXPERTS // CAPACITY_FACTOR
    gwin = 2 * tb // rate + align
    fwin = 2 * sub // rate + align
    wstep = sub
    ywin = wstep + tb + align
    xe, gates = _moe_gather(starts, ends, code, afft, h2, tb, win, gwin, cap + win, tb // sub)
    y = _moe_ffn(xe, gates, w_gate[None], w_up[None], w_down[None], cap, cap + ywin)
    return _moe_scatter(starts, ends, code, y, x1, gt2, tb, ywin, sub, fwin, wstep)


def kernel(x, c, w_ada, b_ada, norm1_w, w_in, q_norm_w, k_norm_w, conv_w, a_log, dt_bias, dn_norm_w,
           w_attn_up, w_dn_up, w_o, norm2_w, w_router, w_gate, w_up, w_down):
    depth = w_ada.shape[0]
    for l in range(depth):
        x = _layer(x, c, w_ada[l], b_ada[l], norm1_w[l], w_in[l], q_norm_w[l], k_norm_w[l], conv_w[l],
                   a_log[l], dt_bias[l], dn_norm_w[l], w_attn_up[l], w_dn_up[l], w_o[l], norm2_w[l],
                   w_router[l], w_gate[l], w_up[l], w_down[l])
    return x
```
